```python
import math
import jax
import jax.numpy as jnp
from jax import lax
import numpy as np

D_MODEL = 1024
BATCH = 1
SEQ = 16384
DEPTH = 2
DEC_BATCH = 128
DEC_SEQ = 4
PAST_LEN = 16384
PAGE_SIZE = 128

HEAD_DIM = 64
CONV_CH = 256
CONV_WIDTH = 31
MLA_HEADS = 4
MLA_Q_LORA = 256
MLA_KV_LORA = 128
MLA_NOPE = 64
MLA_ROPE = 32
MLA_V = 64
MLA_SCALE = (MLA_NOPE + MLA_ROPE) ** -0.5
ROPE_THETA = 10000.0
SB_HEADS = 4
SB_KV_HEADS = 2
SB_SCALE = HEAD_DIM ** -0.5
MB_HEADS = 4
MB_KV_HEADS = 2
MB_BLOCK = 256
MB_TOPK = 3
MB_SCALE = HEAD_DIM ** -0.5
T5_BUCKETS = 32
T5_MAX_DIST = 128
Q_BLOCK = 128
N_BRANCH = 4
N_EXPERTS = 64
EXPERT_FF = 256
SHARED_FF = 256
TOP_K = 8
N_GROUPS = 8
TOPK_GROUPS = 4
ROUTED_SCALE = 2.5
MOE_CHUNK = 512
ALPHA = (2 * DEPTH) ** 0.25
BETA = (8 * DEPTH) ** -0.25
LN_EPS = 1e-5

IN_SPLITS = (2 * CONV_CH, MLA_Q_LORA, MLA_KV_LORA, MLA_ROPE,
             SB_HEADS * HEAD_DIM, SB_KV_HEADS * HEAD_DIM, SB_KV_HEADS * HEAD_DIM,
             MB_HEADS * HEAD_DIM, MB_KV_HEADS * HEAD_DIM, MB_KV_HEADS * HEAD_DIM,
             N_BRANCH * D_MODEL)
D_IN = sum(IN_SPLITS)

kernel_name = 'hybrid_gated_conv_mla_sb_moba_moe_step'


def _layer_norm(x, g, b):
    xf = x.astype(jnp.float32)
    mu = jnp.mean(xf, axis=-1, keepdims=True)
    var = jnp.mean(jnp.square(xf - mu), axis=-1, keepdims=True)
    return ((xf - mu) * lax.rsqrt(var + LN_EPS)).astype(x.dtype) * g + b


def _rms_norm(x, g):
    xf = x.astype(jnp.float32)
    return (xf * lax.rsqrt(jnp.mean(jnp.square(xf), axis=-1, keepdims=True) + LN_EPS)).astype(x.dtype) * g


def _rope(x, pos):
    half = x.shape[-1] // 2
    inv = ROPE_THETA ** (-jnp.arange(half, dtype=jnp.float32) / half)
    ang = pos.astype(jnp.float32)[:, None] * inv[None, :]
    shape = (pos.shape[0],) + (1,) * (x.ndim - 3) + (half,)
    cos = jnp.cos(ang).reshape(shape).astype(x.dtype)
    sin = jnp.sin(ang).reshape(shape).astype(x.dtype)
    x1, x2 = x[..., :half], x[..., half:]
    return jnp.concatenate([x1 * cos - x2 * sin, x1 * sin + x2 * cos], axis=-1)


def _t5_bucket(rel):
    n = jnp.maximum(rel, 0)
    exact = T5_BUCKETS // 2
    nf = jnp.maximum(n, 1).astype(jnp.float32)
    large = exact + (jnp.log(nf / exact) / math.log(T5_MAX_DIST / exact)
                     * (T5_BUCKETS - exact)).astype(jnp.int32)
    large = jnp.minimum(large, T5_BUCKETS - 1)
    return jnp.where(n < exact, n, large)


def _project(h, pos, lp):
    B, T, _ = h.shape
    u = h @ lp['w_in']
    points = np.cumsum(IN_SPLITS)[:-1].tolist()
    cv, cq, ckv, kpe, sq, sk, sv, mq, mk, mv, gate = jnp.split(u, points, axis=-1)
    a, b = jnp.split(cv, 2, axis=-1)
    glu = a * jax.nn.sigmoid(b)
    cq = _rms_norm(cq, lp['mla_q_norm'])
    q = (cq @ lp['w_q_up']).reshape(B, T, MLA_HEADS, MLA_NOPE + MLA_ROPE)
    q_nope = q[..., :MLA_NOPE]
    q_pe = _rope(q[..., MLA_NOPE:], pos)
    w_kv = lp['w_kv_up'].reshape(MLA_KV_LORA, MLA_HEADS, MLA_NOPE + MLA_V)
    q_lat = jnp.einsum('bthn,rhn->bthr', q_nope, w_kv[..., :MLA_NOPE])
    ckv = _rms_norm(ckv, lp['mla_kv_norm'])
    kpe = _rope(kpe, pos)
    return {
        'glu': glu, 'q_lat': q_lat, 'q_pe': q_pe, 'ckv': ckv, 'kpe': kpe,
        'sq': sq.reshape(B, T, SB_HEADS, HEAD_DIM),
        'sk': sk.reshape(B, T, SB_KV_HEADS, HEAD_DIM),
        'sv': sv.reshape(B, T, SB_KV_HEADS, HEAD_DIM),
        'mq': mq.reshape(B, T, MB_HEADS, HEAD_DIM),
        'mk': mk.reshape(B, T, MB_KV_HEADS, HEAD_DIM),
        'mv': mv.reshape(B, T, MB_KV_HEADS, HEAD_DIM),
        'gate': gate,
    }


def _mla_core(q_lat, q_pe, ckv, kpe, qpos):
    s = jnp.einsum('qhr,kr->hqk', q_lat, ckv) + jnp.einsum('qhp,kp->hqk', q_pe, kpe)
    s = s.astype(jnp.float32) * MLA_SCALE
    mask = jnp.arange(ckv.shape[0])[None, :] <= qpos[:, None]
    p = jax.nn.softmax(jnp.where(mask, s, -jnp.inf), axis=-1)
    return jnp.einsum('hqk,kr->qhr', p.astype(ckv.dtype), ckv)


def _sb_core(q, k, v, qpos):
    Tq, H, D = q.shape
    Tk, G, _ = k.shape
    qg = q.reshape(Tq, G, H // G, D)
    z = jnp.einsum('qgrd,kgd->grqk', qg, k).astype(jnp.float32) * SB_SCALE
    mask = jnp.arange(Tk)[None, :] < qpos[:, None]
    log1m = jnp.where(mask, jax.nn.log_sigmoid(-z), 0.0)
    after = lax.cumsum(log1m, axis=3, reverse=True) - log1m
    a = jnp.where(mask, jnp.exp(jax.nn.log_sigmoid(z) + after), 0.0)
    o = jnp.einsum('grqk,kgd->qgrd', a.astype(v.dtype), v)
    return o.reshape(Tq, H, D)


def _moba_blocks(k, v):
    Tk, G, D = k.shape
    nb = -(-Tk // MB_BLOCK)
    pad = nb * MB_BLOCK - Tk

    def blk(a):
        return jnp.pad(a, ((0, pad), (0, 0), (0, 0))).reshape(nb, MB_BLOCK, G, D).transpose(2, 0, 1, 3)

    kb, vb = blk(k), blk(v)
    kmean = jnp.mean(kb.astype(jnp.float32), axis=2).astype(k.dtype)
    return kb, vb, kmean


def _moba_core(q, qpos, kb, vb, kmean, t5_bias):
    Tq, H, D = q.shape
    G, NB, BS, _ = kb.shape
    kvh = jnp.arange(H) // (H // G)
    own = qpos // MB_BLOCK
    blk_score = jnp.einsum('qhd,hnd->qhn', q, kmean[kvh]).astype(jnp.float32)
    past = jnp.arange(NB)[None, None, :] < own[:, None, None]
    top_s, top_i = lax.top_k(jnp.where(past, blk_score, -jnp.inf), min(MB_TOPK, NB))
    blocks = jnp.concatenate([top_i, jnp.broadcast_to(own[:, None, None], (Tq, H, 1))], axis=-1)
    blk_ok = jnp.concatenate([jnp.isfinite(top_s), jnp.ones((Tq, H, 1), bool)], axis=-1)
    kg = kb[kvh[None, :, None], blocks]
    vg = vb[kvh[None, :, None], blocks]
    kpos = blocks[..., None] * MB_BLOCK + jnp.arange(MB_BLOCK)
    rel = qpos[:, None, None, None] - kpos
    bias = t5_bias[_t5_bucket(rel), jnp.arange(H)[None, :, None, None]]
    logits = jnp.einsum('qhd,qhnsd->qhns', q, kg).astype(jnp.float32) * MB_SCALE + bias.astype(jnp.float32)
    mask = blk_ok[..., None] & (rel >= 0)
    ns = blocks.shape[-1]
    p = jax.nn.softmax(jnp.where(mask, logits, -jnp.inf).reshape(Tq, H, ns * BS), axis=-1)
    p = p.reshape(Tq, H, ns, BS)
    return jnp.einsum('qhns,qhnsd->qhd', p.astype(vb.dtype), vg)


def _prompt_attention(pc, t5_bias):
    T = pc['q_lat'].shape[1]
    n_qb = T // Q_BLOCK
    qpos = jnp.arange(T, dtype=jnp.int32)

    def one_seq(q_lat, q_pe, ckv, kpe, sq, sk, sv, mq, mk, mv):
        kb, vb, kmean = _moba_blocks(mk, mv)

        def block(args):
            ql, qp, s_q, m_q, pos_b = args
            return (_mla_core(ql, qp, ckv, kpe, pos_b),
                    _sb_core(s_q, sk, sv, pos_b),
                    _moba_core(m_q, pos_b, kb, vb, kmean, t5_bias))

        xs = tuple(a.reshape((n_qb, Q_BLOCK) + a.shape[1:]) for a in (q_lat, q_pe, sq, mq, qpos))
        outs = lax.map(block, xs)
        return tuple(o.reshape((T,) + o.shape[2:]) for o in outs)

    return jax.vmap(one_seq)(pc['q_lat'], pc['q_pe'], pc['ckv'], pc['kpe'], pc['sq'], pc['sk'],
                             pc['sv'], pc['mq'], pc['mk'], pc['mv'])


def _sample_attention(pc, layer, page_table, cache_mla_ckv, cache_mla_kpe, cache_sb_k, cache_sb_v,
                      cache_moba_k, cache_moba_v, t5_bias):
    Tq = pc['q_lat'].shape[1]
    qpos = PAST_LEN + jnp.arange(Tq, dtype=jnp.int32)

    def one_seq(args):
        pt, q_lat, q_pe, ckv_n, kpe_n, sq, sk_n, sv_n, mq, mk_n, mv_n = args

        def with_past(cache, new):
            past = cache[layer, pt].reshape((-1,) + cache.shape[3:])
            return jnp.concatenate([past, new], axis=0)

        ckv = with_past(cache_mla_ckv, ckv_n)
        kpe = with_past(cache_mla_kpe, kpe_n)
        sk = with_past(cache_sb_k, sk_n)
        sv = with_past(cache_sb_v, sv_n)
        mk = with_past(cache_moba_k, mk_n)
        mv = with_past(cache_moba_v, mv_n)
        kb, vb, kmean = _moba_blocks(mk, mv)
        return (_mla_core(q_lat, q_pe, ckv, kpe, qpos),
                _sb_core(sq, sk, sv, qpos),
                _moba_core(mq, qpos, kb, vb, kmean, t5_bias))

    return lax.map(one_seq, (page_table, pc['q_lat'], pc['q_pe'], pc['ckv'], pc['kpe'], pc['sq'],
                             pc['sk'], pc['sv'], pc['mq'], pc['mk'], pc['mv']))


def _depthwise_conv(x, w, b):
    C = x.shape[-1]
    y = lax.conv_general_dilated(x, w[:, None, :], window_strides=(1,), padding='VALID',
                                 dimension_numbers=('NWC', 'WIO', 'NWC'), feature_group_count=C)
    return y + b


def _merge(pc, conv_full, attn, lp):
    B, T = pc['glu'].shape[:2]
    c = _depthwise_conv(conv_full, lp['conv_w'], lp['conv_b'])
    c = jax.nn.silu(_layer_norm(c, lp['conv_ln_g'], lp['conv_ln_b'])) @ lp['w_conv_out']
    mla_lat, sb_o, mb_o = attn
    w_uv = lp['w_kv_up'].reshape(MLA_KV_LORA, MLA_HEADS, MLA_NOPE + MLA_V)[..., MLA_NOPE:]
    m = jnp.einsum('bthr,rhv->bthv', mla_lat, w_uv).reshape(B, T, -1) @ lp['w_mla_out']
    s = sb_o.reshape(B, T, -1) @ lp['w_sb_out']
    o = mb_o.reshape(B, T, -1) @ lp['w_mb_out']
    g = jax.nn.sigmoid(pc['gate'] + lp['gate_b']).reshape(B, T, N_BRANCH, D_MODEL)
    y = g[:, :, 0] * c + g[:, :, 1] * m + g[:, :, 2] * s + g[:, :, 3] * o
    return y @ lp['w_out']


def _moe_chunk(xc, lp):
    f32 = jnp.float32
    scores = jax.nn.sigmoid((xc @ lp['router_w']).astype(f32))
    biased = scores + lp['router_bias'].astype(f32)
    grp = biased.reshape(-1, N_GROUPS, N_EXPERTS // N_GROUPS)
    g_score = jnp.sum(lax.top_k(grp, 2)[0], axis=-1)
    _, g_idx = lax.top_k(g_score, TOPK_GROUPS)
    g_keep = jnp.sum(jax.nn.one_hot(g_idx, N_GROUPS, dtype=f32), axis=-2) > 0
    e_keep = jnp.repeat(g_keep, N_EXPERTS // N_GROUPS, axis=-1)
    _, idx = lax.top_k(jnp.where(e_keep, biased, -jnp.inf), TOP_K)
    sel = jnp.take_along_axis(scores, idx, axis=-1)
    wts = sel / jnp.sum(sel, axis=-1, keepdims=True) * ROUTED_SCALE
    gate = jnp.sum(jax.nn.one_hot(idx, N_EXPERTS, dtype=f32) * wts[..., None], axis=-2).astype(xc.dtype)
    hg = jnp.einsum('nd,edf->nef', xc, lp['w_e_gate'])
    hu = jnp.einsum('nd,edf->nef', xc, lp['w_e_up'])
    h = jax.nn.silu(hg) * hu * gate[..., None]
    routed = jnp.einsum('nef,efd->nd', h, lp['w_e_down'])
    shared = (jax.nn.silu(xc @ lp['w_s_gate']) * (xc @ lp['w_s_up'])) @ lp['w_s_down']
    return routed + shared


def _moe(x, lp):
    B, T, D = x.shape
    n = B * T
    chunk = min(MOE_CHUNK, n)
    n_chunks = -(-n // chunk)
    xf = jnp.pad(x.reshape(n, D), ((0, n_chunks * chunk - n), (0, 0))).reshape(n_chunks, chunk, D)
    out = lax.map(lambda xc: _moe_chunk(xc, lp), xf)
    return out.reshape(-1, D)[:n].reshape(B, T, D)


def _finish_layer(x, pc, conv_full, attn, lp):
    y = _merge(pc, conv_full, attn, lp)
    x = _layer_norm(ALPHA * x + y, lp['ln1_g'], lp['ln1_b'])
    return _layer_norm(ALPHA * x + _moe(x, lp), lp['ln2_g'], lp['ln2_b'])


def setup_inputs(seed: int = 0) -> dict:
    key = jax.random.key(seed)
    ks = iter(jax.random.split(key, 48))
    f32 = jnp.float32
    n_pages = PAST_LEN // PAGE_SIZE
    n_pool = (DEC_BATCH * n_pages * 5) // 4
    L = DEPTH

    def nrm(shape, scale=1.0):
        return jax.random.normal(next(ks), shape, f32) * scale

    def gain(shape):
        return 1.0 + nrm(shape, 0.02)

    return {
        'x_prompt': nrm((BATCH, SEQ, D_MODEL)),
        'x_sample': nrm((DEC_BATCH, DEC_SEQ, D_MODEL)),
        'cache_mla_ckv': nrm((L, n_pool, PAGE_SIZE, MLA_KV_LORA)),
        'cache_mla_kpe': nrm((L, n_pool, PAGE_SIZE, MLA_ROPE)),
        'cache_sb_k': nrm((L, n_pool, PAGE_SIZE, SB_KV_HEADS, HEAD_DIM)),
        'cache_sb_v': nrm((L, n_pool, PAGE_SIZE, SB_KV_HEADS, HEAD_DIM)),
        'cache_moba_k': nrm((L, n_pool, PAGE_SIZE, MB_KV_HEADS, HEAD_DIM)),
        'cache_moba_v': nrm((L, n_pool, PAGE_SIZE, MB_KV_HEADS, HEAD_DIM)),
        'state_conv': nrm((L, DEC_BATCH, CONV_WIDTH - 1, CONV_CH)),
        'page_table': jax.random.permutation(next(ks), n_pool)[:DEC_BATCH * n_pages]
                      .reshape(DEC_BATCH, n_pages).astype(jnp.int32),
        't5_bias': nrm((T5_BUCKETS, MB_HEADS), 0.3),
        'w_in': nrm((L, D_MODEL, D_IN), D_MODEL ** -0.5),
        'gate_b': nrm((L, N_BRANCH * D_MODEL), 0.02),
        'conv_w': nrm((L, CONV_WIDTH, CONV_CH), CONV_WIDTH ** -0.5),
        'conv_b': nrm((L, CONV_CH), 0.02),
        'conv_ln_g': gain((L, CONV_CH)),
        'conv_ln_b': nrm((L, CONV_CH), 0.02),
        'w_conv_out': nrm((L, CONV_CH, D_MODEL), CONV_CH ** -0.5),
        'mla_q_norm': gain((L, MLA_Q_LORA)),
        'w_q_up': nrm((L, MLA_Q_LORA, MLA_HEADS * (MLA_NOPE + MLA_ROPE)), MLA_Q_LORA ** -0.5),
        'mla_kv_norm': gain((L, MLA_KV_LORA)),
        'w_kv_up': nrm((L, MLA_KV_LORA, MLA_HEADS * (MLA_NOPE + MLA_V)), MLA_KV_LORA ** -0.5),
        'w_mla_out': nrm((L, MLA_HEADS * MLA_V, D_MODEL), (MLA_HEADS * MLA_V) ** -0.5),
        'w_sb_out': nrm((L, SB_HEADS * HEAD_DIM, D_MODEL), (SB_HEADS * HEAD_DIM) ** -0.5),
        'w_mb_out': nrm((L, MB_HEADS * HEAD_DIM, D_MODEL), (MB_HEADS * HEAD_DIM) ** -0.5),
        'w_out': nrm((L, D_MODEL, D_MODEL), D_MODEL ** -0.5 * BETA),
        'ln1_g': gain((L, D_MODEL)),
        'ln1_b': nrm((L, D_MODEL), 0.02),
        'router_w': nrm((L, D_MODEL, N_EXPERTS), D_MODEL ** -0.5),
        'router_bias': nrm((L, N_EXPERTS), 0.01),
        'w_e_gate': nrm((L, N_EXPERTS, D_MODEL, EXPERT_FF), D_MODEL ** -0.5),
        'w_e_up': nrm((L, N_EXPERTS, D_MODEL, EXPERT_FF), D_MODEL ** -0.5),
        'w_e_down': nrm((L, N_EXPERTS, EXPERT_FF, D_MODEL), EXPERT_FF ** -0.5 * BETA),
        'w_s_gate': nrm((L, D_MODEL, SHARED_FF), D_MODEL ** -0.5),
        'w_s_up': nrm((L, D_MODEL, SHARED_FF), D_MODEL ** -0.5),
        'w_s_down': nrm((L, SHARED_FF, D_MODEL), SHARED_FF ** -0.5 * BETA),
        'ln2_g': gain((L, D_MODEL)),
        'ln2_b': nrm((L, D_MODEL), 0.02),
    }


def reference(x_prompt, x_sample, cache_mla_ckv, cache_mla_kpe, cache_sb_k, cache_sb_v,
              cache_moba_k, cache_moba_v, state_conv, page_table, t5_bias,
              w_in, gate_b, conv_w, conv_b, conv_ln_g, conv_ln_b, w_conv_out,
              mla_q_norm, w_q_up, mla_kv_norm, w_kv_up, w_mla_out, w_sb_out, w_mb_out,
              w_out, ln1_g, ln1_b, router_w, router_bias, w_e_gate, w_e_up, w_e_down,
              w_s_gate, w_s_up, w_s_down, ln2_g, ln2_b):
    x_p, x_s = x_prompt, x_sample
    pos_p = jnp.arange(x_p.shape[1], dtype=jnp.int32)
    pos_s = PAST_LEN + jnp.arange(x_s.shape[1], dtype=jnp.int32)
    names = ('ckv', 'kpe', 'sk', 'sv', 'mk', 'mv')
    new_p = {n: [] for n in names + ('conv',)}
    new_s = {n: [] for n in names + ('conv',)}
    for l in range(DEPTH):
        lp = {
            'w_in': w_in[l], 'gate_b': gate_b[l], 'conv_w': conv_w[l], 'conv_b': conv_b[l],
            'conv_ln_g': conv_ln_g[l], 'conv_ln_b': conv_ln_b[l], 'w_conv_out': w_conv_out[l],
            'mla_q_norm': mla_q_norm[l], 'w_q_up': w_q_up[l], 'mla_kv_norm': mla_kv_norm[l],
            'w_kv_up': w_kv_up[l], 'w_mla_out': w_mla_out[l], 'w_sb_out': w_sb_out[l],
            'w_mb_out': w_mb_out[l], 'w_out': w_out[l], 'ln1_g': ln1_g[l], 'ln1_b': ln1_b[l],
            'router_w': router_w[l], 'router_bias': router_bias[l], 'w_e_gate': w_e_gate[l],
            'w_e_up': w_e_up[l], 'w_e_down': w_e_down[l], 'w_s_gate': w_s_gate[l],
            'w_s_up': w_s_up[l], 'w_s_down': w_s_down[l], 'ln2_g': ln2_g[l], 'ln2_b': ln2_b[l],
        }
        pc = _project(x_p, pos_p, lp)
        conv_p = jnp.pad(pc['glu'], ((0, 0), (CONV_WIDTH - 1, 0), (0, 0)))
        attn_p = _prompt_attention(pc, t5_bias)
        sc = _project(x_s, pos_s, lp)
        conv_s = jnp.concatenate([state_conv[l], sc['glu']], axis=1)
        attn_s = _sample_attention(sc, l, page_table, cache_mla_ckv, cache_mla_kpe, cache_sb_k,
                                   cache_sb_v, cache_moba_k, cache_moba_v, t5_bias)
        for n in names:
            new_p[n].append(pc[n])
            new_s[n].append(sc[n])
        new_p['conv'].append(conv_p[:, -(CONV_WIDTH - 1):])
        new_s['conv'].append(conv_s[:, -(CONV_WIDTH - 1):])
        x_p = _finish_layer(x_p, pc, conv_p, attn_p, lp)
        x_s = _finish_layer(x_s, sc, conv_s, attn_s, lp)
    return (x_p, x_s,
            jnp.stack(new_p['ckv']), jnp.stack(new_s['ckv']),
            jnp.stack(new_p['kpe']), jnp.stack(new_s['kpe']),
            jnp.stack(new_p['sk']), jnp.stack(new_s['sk']),
            jnp.stack(new_p['sv']), jnp.stack(new_s['sv']),
            jnp.stack(new_p['mk']), jnp.stack(new_s['mk']),
            jnp.stack(new_p['mv']), jnp.stack(new_s['mv']),
            jnp.stack(new_p['conv']), jnp.stack(new_s['conv']))
```

```python
import functools
import math

import numpy as np
import jax
import jax.numpy as jnp
from jax import lax
from jax.experimental import pallas as pl
from jax.experimental.pallas import tpu as pltpu

F32 = jnp.float32
BF16 = jnp.bfloat16

D_MODEL = 1024
HEAD_DIM = 64
CONV_CH = 256
CONV_WIDTH = 31
MLA_HEADS = 4
MLA_Q_LORA = 256
MLA_KV_LORA = 128
MLA_NOPE = 64
MLA_ROPE = 32
MLA_V = 64
MLA_SCALE = (MLA_NOPE + MLA_ROPE) ** -0.5
ROPE_THETA = 10000.0
N_HEADS = 4
KV_HEADS = 2
QK_SCALE = HEAD_DIM ** -0.5
MB_BLOCK = 256
MB_TOPK = 3
T5_BUCKETS = 32
T5_MAX_DIST = 128
N_BRANCH = 4
N_EXPERTS = 64
EXPERT_FF = 256
SHARED_FF = 256
TOP_K = 8
N_GROUPS = 8
TOPK_GROUPS = 4
ROUTED_SCALE = 2.5
LN_EPS = 1e-5
PAGE_SIZE = 128

LANE = 128
NEG = -1e30
VMEM_LIMIT = 56 * 1024 * 1024

NT_DIMS = (((1,), (1,)), ((), ()))


def _cparams(*sem):
    return pltpu.CompilerParams(dimension_semantics=sem, vmem_limit_bytes=VMEM_LIMIT)


def _pick_tile(n, cands):
    for c in cands:
        if n % c == 0:
            return c
    raise ValueError(f"no tile for {n}")


def _full(shape):
    nd = len(shape)
    return pl.BlockSpec(shape, lambda *_: (0,) * nd)


def _dot(a, b):
    return jnp.dot(a, b, preferred_element_type=F32)


def _dot_nt(a, b):
    return lax.dot_general(a, b, NT_DIMS, preferred_element_type=F32)


def _split_bf16(x):
    hi = x.astype(BF16)
    lo = (x - hi.astype(F32)).astype(BF16)
    return hi, lo


def _layer_norm(x, g, b):
    mu = jnp.mean(x, axis=-1, keepdims=True)
    d = x - mu
    var = jnp.mean(d * d, axis=-1, keepdims=True)
    return d * lax.rsqrt(var + LN_EPS) * g + b


def _rms_norm(x, g):
    return x * lax.rsqrt(jnp.mean(x * x, axis=-1, keepdims=True) + LN_EPS) * g


def _silu(x):
    return x * jax.nn.sigmoid(x)


def _head_pad_cols(w):
    k = w.shape[0]
    wh = w.reshape(k, N_HEADS, HEAD_DIM)
    z = jnp.zeros_like(wh)
    per_head = [jnp.concatenate([wh[:, h], z[:, h]] if h // 2 == 0 else [z[:, h], wh[:, h]], axis=-1)
                for h in range(N_HEADS)]
    return jnp.concatenate(per_head, axis=-1)


def _rot_cols(w):
    half = w.shape[-1] // 2
    return jnp.concatenate([-w[..., half:], w[..., :half]], axis=-1)


def _pad_cols(w, width):
    return jnp.pad(w, ((0, 0), (0, width - w.shape[-1])))


C_CONV, C_CQ, C_CKV, C_SQ, C_SK, C_SV, C_MQ, C_MK, C_MV, C_KPE, C_KROT, C_END = (
    0, 512, 768, 896, 1408, 1536, 1664, 2176, 2304, 2432, 2560, 2688)


def _pack_layer_weights(w_in, w_q_up, w_kv_up):
    pts = np.cumsum((2 * CONV_CH, MLA_Q_LORA, MLA_KV_LORA, MLA_ROPE, 256, 128, 128, 256, 128, 128))
    cv, cq, ckv, kpe, sq, sk, sv, mq, mk, mv, gate = jnp.split(w_in, pts.tolist(), axis=-1)
    w1 = jnp.concatenate([
        cv, cq, ckv, _head_pad_cols(sq), sk, sv, _head_pad_cols(mq), mk, mv,
        _pad_cols(kpe, LANE), _pad_cols(_rot_cols(kpe), LANE)], axis=-1).astype(BF16)
    wq = w_q_up.reshape(MLA_Q_LORA, MLA_HEADS, MLA_NOPE + MLA_ROPE)
    nope = wq[:, :, :MLA_NOPE].reshape(MLA_Q_LORA, MLA_HEADS * MLA_NOPE)
    pe = [_pad_cols(wq[:, h, MLA_NOPE:], LANE) for h in range(MLA_HEADS)]
    rot = [_pad_cols(_rot_cols(wq[:, h, MLA_NOPE:]), LANE) for h in range(MLA_HEADS)]
    wq_p = jnp.concatenate([nope] + pe + rot, axis=-1).astype(BF16)
    wkv = w_kv_up.reshape(MLA_KV_LORA, MLA_HEADS, MLA_NOPE + MLA_V)
    wuk = jnp.zeros((MLA_HEADS * MLA_NOPE, MLA_HEADS * MLA_KV_LORA), F32)
    wuv = jnp.zeros((MLA_HEADS * MLA_KV_LORA, MLA_HEADS * MLA_V), F32)
    for h in range(MLA_HEADS):
        wuk = wuk.at[h * MLA_NOPE:(h + 1) * MLA_NOPE, h * MLA_KV_LORA:(h + 1) * MLA_KV_LORA].set(
            wkv[:, h, :MLA_NOPE].T)
        wuv = wuv.at[h * MLA_KV_LORA:(h + 1) * MLA_KV_LORA, h * MLA_V:(h + 1) * MLA_V].set(
            wkv[:, h, MLA_NOPE:])
    return w1, gate.astype(BF16), wq_p, wuk.astype(BF16), wuv.astype(BF16)


def _rope_tables(pos):
    half = MLA_ROPE // 2
    inv = ROPE_THETA ** (-jnp.arange(half, dtype=F32) / half)
    ang = pos.astype(F32)[:, None] * inv[None, :]
    cos = jnp.tile(jnp.cos(ang), (1, 2 * MLA_HEADS))
    sin = jnp.tile(jnp.sin(ang), (1, 2 * MLA_HEADS))
    return cos, sin


def _t5_bucket(rel):
    n = jnp.maximum(rel, 0)
    exact = T5_BUCKETS // 2
    nf = jnp.maximum(n, 1).astype(F32)
    large = exact + (jnp.log(nf / exact) / math.log(T5_MAX_DIST / exact)
                     * (T5_BUCKETS - exact)).astype(jnp.int32)
    large = jnp.minimum(large, T5_BUCKETS - 1)
    return jnp.where(n < exact, n, large)


def _proj_kernel(x_ref, cos_ref, sin_ref, w1_ref, wg_ref, wq_ref, wuk_ref, qn_ref, kvn_ref,
                 glu_ref, ckv_ref, kpe_ref, kcat_ref, qcat_ref, sq_ref, sk_ref, sv_ref, skv_ref,
                 mqh_ref, mql_ref, mk_ref, mv_ref, mkv_ref, gate_ref):
    x = x_ref[...].astype(BF16)
    u = _dot(x, w1_ref[...])
    gate_ref[...] = _dot(x, wg_ref[...])
    cos = cos_ref[...]
    sin = sin_ref[...]
    glu_ref[...] = u[:, C_CONV:C_CONV + CONV_CH] * jax.nn.sigmoid(u[:, C_CONV + CONV_CH:C_CQ])
    ckvn = _rms_norm(u[:, C_CKV:C_SQ], kvn_ref[...])
    ckv_ref[...] = ckvn
    kpe = u[:, C_KPE:C_KROT] * cos + u[:, C_KROT:C_END] * sin
    kpe_ref[...] = kpe[:, :MLA_ROPE]
    kcat_ref[...] = jnp.concatenate([ckvn, kpe], axis=-1).astype(BF16)
    cqn = _rms_norm(u[:, C_CQ:C_CKV], qn_ref[...])
    qa = _dot(cqn.astype(BF16), wq_ref[...])
    nq = MLA_HEADS * MLA_NOPE
    qlat = _dot(qa[:, :nq].astype(BF16), wuk_ref[...])
    parts = []
    for h in range(MLA_HEADS):
        pe = (qa[:, nq + LANE * h:nq + LANE * (h + 1)] * cos
              + qa[:, nq + LANE * (MLA_HEADS + h):nq + LANE * (MLA_HEADS + h + 1)] * sin)
        parts += [qlat[:, MLA_KV_LORA * h:MLA_KV_LORA * (h + 1)], pe]
    qcat_ref[...] = jnp.concatenate(parts, axis=-1).astype(BF16)
    sq_ref[...] = (u[:, C_SQ:C_SK] * QK_SCALE).astype(BF16)
    sk = u[:, C_SK:C_SV]
    sv = u[:, C_SV:C_MQ]
    sk_ref[...] = sk
    sv_ref[...] = sv
    skv_ref[...] = jnp.concatenate([sk, sv], axis=-1).astype(BF16)
    mqh, mql = _split_bf16(u[:, C_MQ:C_MK] * QK_SCALE)
    mqh_ref[...] = mqh
    mql_ref[...] = mql
    mk = u[:, C_MK:C_MV]
    mv = u[:, C_MV:C_KPE]
    mk_ref[...] = mk
    mv_ref[...] = mv
    mkv_ref[...] = jnp.concatenate([mk, mv], axis=-1).astype(BF16)


def _project(x, cos, sin, w1, wg, wq, wuk, qn, kvn):
    n = x.shape[0]
    tm = _pick_tile(n, (256, 128, 64, 32, 16, 8))
    row = lambda w: pl.BlockSpec((tm, w), lambda i: (i, 0))
    outs = [("glu", CONV_CH, F32), ("ckv", MLA_KV_LORA, F32), ("kpe", MLA_ROPE, F32),
            ("kcat", 2 * LANE, BF16), ("qcat", MLA_HEADS * 2 * LANE, BF16),
            ("sq", N_HEADS * LANE, BF16), ("sk", LANE, F32), ("sv", LANE, F32), ("skv", 2 * LANE, BF16),
            ("mqh", N_HEADS * LANE, BF16), ("mql", N_HEADS * LANE, BF16),
            ("mk", LANE, F32), ("mv", LANE, F32), ("mkv", 2 * LANE, BF16),
            ("gate", N_BRANCH * D_MODEL, F32)]
    res = pl.pallas_call(
        _proj_kernel,
        grid=(n // tm,),
        in_specs=[row(D_MODEL), row(LANE), row(LANE), _full(w1.shape), _full(wg.shape), _full(wq.shape),
                  _full(wuk.shape), _full(qn.shape), _full(kvn.shape)],
        out_specs=[row(w) for _, w, _ in outs],
        out_shape=[jax.ShapeDtypeStruct((n, w), dt) for _, w, dt in outs],
        compiler_params=_cparams("parallel"),
        name="proj",
    )(x, cos, sin, w1, wg, wq, wuk, qn, kvn)
    return {name: r for (name, _, _), r in zip(outs, res)}


def _stack_heads(dst_ref, src_ref, tq, width):
    for h in range(N_HEADS):
        dst_ref[h * tq:(h + 1) * tq, :] = src_ref[:, h * width:(h + 1) * width]


def _local_row_col(rows, tq):
    row = lax.broadcasted_iota(jnp.int32, (rows, tq), 0) & (tq - 1)
    col = lax.broadcasted_iota(jnp.int32, (rows, tq), 1)
    return row, col


def _softmax_update(s, v, m_ref, l_ref, acc_ref):
    m_prev = m_ref[...]
    m_new = jnp.maximum(m_prev, jnp.max(s, axis=-1, keepdims=True))
    p = jnp.exp(s - m_new)
    alpha = jnp.exp(m_prev - m_new)
    l_ref[...] = alpha * l_ref[...] + jnp.sum(p, axis=-1, keepdims=True)
    acc_ref[...] = alpha * acc_ref[...] + _dot(p.astype(BF16), v)
    m_ref[...] = m_new


def _init_softmax(m_ref, l_ref, acc_ref):
    m_ref[...] = jnp.full(m_ref.shape, NEG, F32)
    l_ref[...] = jnp.zeros(l_ref.shape, F32)
    acc_ref[...] = jnp.zeros(acc_ref.shape, F32)


def _mla_prompt_kernel(q_ref, k_ref, o_ref, qs_ref, m_ref, l_ref, acc_ref, *, tq):
    i = pl.program_id(0)
    rows = MLA_HEADS * tq
    _stack_heads(qs_ref, q_ref, tq, 2 * LANE)
    _init_softmax(m_ref, l_ref, acc_ref)

    def step(j, masked):
        k = k_ref[pl.ds(pl.multiple_of(j * tq, tq), tq), :]
        s = _dot_nt(qs_ref[...], k) * MLA_SCALE
        if masked:
            row, col = _local_row_col(rows, tq)
            s = jnp.where(col <= row, s, NEG)
        _softmax_update(s, k[:, :MLA_KV_LORA], m_ref, l_ref, acc_ref)

    def body(j, carry):
        step(j, False)
        return carry

    lax.fori_loop(0, i, body, 0)
    step(i, True)
    out = acc_ref[...] / l_ref[...]
    for h in range(MLA_HEADS):
        o_ref[:, h * MLA_KV_LORA:(h + 1) * MLA_KV_LORA] = out[h * tq:(h + 1) * tq]


def _mla_prompt(qcat, kcat, t_p, tq):
    rows = MLA_HEADS * tq
    return pl.pallas_call(
        functools.partial(_mla_prompt_kernel, tq=tq),
        grid=(t_p // tq,),
        in_specs=[pl.BlockSpec((tq, MLA_HEADS * 2 * LANE), lambda i: (i, 0)),
                  pl.BlockSpec((t_p, 2 * LANE), lambda i: (0, 0))],
        out_specs=pl.BlockSpec((tq, MLA_HEADS * MLA_KV_LORA), lambda i: (i, 0)),
        out_shape=jax.ShapeDtypeStruct((t_p, MLA_HEADS * MLA_KV_LORA), F32),
        scratch_shapes=[pltpu.VMEM((rows, 2 * LANE), BF16), pltpu.VMEM((rows, 1), F32),
                        pltpu.VMEM((rows, 1), F32), pltpu.VMEM((rows, MLA_KV_LORA), F32)],
        compiler_params=_cparams("parallel"),
        name="mla_prompt",
    )(qcat, kcat)


def _log_sigmoid_pair(z):
    ls = jnp.minimum(z, 0.0) - jnp.log1p(jnp.exp(-jnp.abs(z)))
    return ls, ls - z


def _suffix_sum(l1m, uu):
    hi, lo = _split_bf16(l1m)
    return _dot(jnp.concatenate([hi, lo], axis=-1), uu)


def _write_group_lanes(o_ref, out, tq):
    lane = lax.broadcasted_iota(jnp.int32, (tq, LANE), 1)
    for h in range(N_HEADS):
        g = h // (N_HEADS // KV_HEADS)
        own = (lane >= HEAD_DIM * g) & (lane < HEAD_DIM * (g + 1))
        o_ref[:, h * LANE:(h + 1) * LANE] = jnp.where(own, out[h * tq:(h + 1) * tq], 0.0)


def _sb_prompt_kernel(q_ref, kv_ref, uu_ref, o_ref, qs_ref, c_ref, acc_ref, *, tq):
    i = pl.program_id(0)
    rows = N_HEADS * tq
    _stack_heads(qs_ref, q_ref, tq, LANE)
    c_ref[...] = jnp.zeros(c_ref.shape, F32)
    acc_ref[...] = jnp.zeros(acc_ref.shape, F32)

    def step(j, masked):
        kv = kv_ref[pl.ds(pl.multiple_of(j * tq, tq), tq), :]
        z = _dot_nt(qs_ref[...], kv[:, :LANE])
        ls, l1m = _log_sigmoid_pair(z)
        if masked:
            row, col = _local_row_col(rows, tq)
            mask = col < row
            l1m = jnp.where(mask, l1m, 0.0)
        a = jnp.exp(ls + _suffix_sum(l1m, uu_ref[...]) + c_ref[...])
        if masked:
            a = jnp.where(mask, a, 0.0)
        acc_ref[...] += _dot(a.astype(BF16), kv[:, LANE:])
        c_ref[...] += jnp.sum(l1m, axis=-1, keepdims=True)

    step(i, True)

    def body(t, carry):
        step(i - 1 - t, False)
        return carry

    lax.fori_loop(0, i, body, 0)
    _write_group_lanes(o_ref, acc_ref[...], tq)


def _suffix_matrix(tk):
    u = (np.arange(tk)[:, None] > np.arange(tk)[None, :]).astype(np.float32)
    return jnp.asarray(np.concatenate([u, u], axis=0), dtype=BF16)


def _sb_prompt(sq, skv, t_p, tq):
    rows = N_HEADS * tq
    return pl.pallas_call(
        functools.partial(_sb_prompt_kernel, tq=tq),
        grid=(t_p // tq,),
        in_specs=[pl.BlockSpec((tq, N_HEADS * LANE), lambda i: (i, 0)),
                  pl.BlockSpec((t_p, 2 * LANE), lambda i: (0, 0)),
                  _full((2 * tq, tq))],
        out_specs=pl.BlockSpec((tq, N_HEADS * LANE), lambda i: (i, 0)),
        out_shape=jax.ShapeDtypeStruct((t_p, N_HEADS * LANE), F32),
        scratch_shapes=[pltpu.VMEM((rows, LANE), BF16), pltpu.VMEM((rows, 1), F32),
                        pltpu.VMEM((rows, LANE), F32)],
        compiler_params=_cparams("parallel"),
        name="sb_prompt",
    )(sq, skv, _suffix_matrix(tq))


def _top3_select(score, valid):
    lane = lax.broadcasted_iota(jnp.int32, score.shape, 1).astype(F32)
    s = jnp.where(valid, score, -jnp.inf)
    sel = jnp.zeros(score.shape, F32)
    for _ in range(MB_TOPK):
        mx = jnp.max(s, axis=-1, keepdims=True)
        cand = jnp.where(s == mx, lane, 1e9)
        cand = jnp.where(mx > -jnp.inf, cand, 1e9)
        pick = lane == jnp.min(cand, axis=-1, keepdims=True)
        sel = jnp.where(pick, 1.0, sel)
        s = jnp.where(pick, -jnp.inf, s)
    return sel


def _block_scores(qh, ql, kmean):
    kh, kl = _split_bf16(kmean)
    return _dot_nt(qh, kh) + _dot_nt(qh, kl) + _dot_nt(ql, kh)


def _pick_column(x, j):
    lane = lax.broadcasted_iota(jnp.int32, x.shape, 1)
    return jnp.sum(jnp.where(lane == j, x, 0.0), axis=-1, keepdims=True)


def _moba_prompt_kernel(t5_ref, qh_ref, ql_ref, kv_ref, kmean_ref, bias_ref, o_ref,
                        qs_ref, qls_ref, sel_ref, m_ref, l_ref, acc_ref, *, tq):
    i = pl.program_id(0)
    rows = N_HEADS * tq
    _stack_heads(qs_ref, qh_ref, tq, LANE)
    _stack_heads(qls_ref, ql_ref, tq, LANE)
    _init_softmax(m_ref, l_ref, acc_ref)
    score = _block_scores(qs_ref[...], qls_ref[...], kmean_ref[...])
    blk = lax.broadcasted_iota(jnp.int32, score.shape, 1)
    sel_ref[...] = _top3_select(score, blk < i)
    head = lax.broadcasted_iota(jnp.int32, (rows, 1), 0) // tq
    far_bias = jnp.zeros((rows, 1), F32)
    for h in range(N_HEADS):
        far_bias = jnp.where(head == h, t5_ref[T5_BUCKETS - 1, h], far_bias)

    def load(j):
        kv = kv_ref[pl.ds(pl.multiple_of(j * tq, tq), tq), :]
        return _dot_nt(qs_ref[...], kv[:, :LANE]), kv[:, LANE:]

    s, v = load(i)
    row, col = _local_row_col(rows, tq)
    s = jnp.where(col <= row, s + bias_ref[:, tq:], NEG)
    _softmax_update(s, v, m_ref, l_ref, acc_ref)

    @pl.when(i >= 1)
    def _():
        s, v = load(i - 1)
        keep = _pick_column(sel_ref[...], i - 1) > 0.0
        s = jnp.where(keep, s + bias_ref[:, :tq], NEG)
        _softmax_update(s, v, m_ref, l_ref, acc_ref)

    def body(j, carry):
        s, v = load(j)
        keep = _pick_column(sel_ref[...], j) > 0.0
        s = jnp.where(keep, s + far_bias, NEG)
        _softmax_update(s, v, m_ref, l_ref, acc_ref)
        return carry

    lax.fori_loop(0, i - 1, body, 0)
    _write_group_lanes(o_ref, acc_ref[...] / l_ref[...], tq)


def _moba_prompt(t5_bias, mqh, mql, mkv, kmean, bias, t_p):
    tq = MB_BLOCK
    rows = N_HEADS * tq
    qspec = pl.BlockSpec((tq, N_HEADS * LANE), lambda i: (i, 0))
    return pl.pallas_call(
        functools.partial(_moba_prompt_kernel, tq=tq),
        grid=(t_p // tq,),
        in_specs=[pl.BlockSpec(memory_space=pltpu.SMEM), qspec, qspec,
                  pl.BlockSpec((t_p, 2 * LANE), lambda i: (0, 0)),
                  _full(kmean.shape), _full(bias.shape)],
        out_specs=pl.BlockSpec((tq, N_HEADS * LANE), lambda i: (i, 0)),
        out_shape=jax.ShapeDtypeStruct((t_p, N_HEADS * LANE), F32),
        scratch_shapes=[pltpu.VMEM((rows, LANE), BF16), pltpu.VMEM((rows, LANE), BF16),
                        pltpu.VMEM((rows, kmean.shape[0]), F32), pltpu.VMEM((rows, 1), F32),
                        pltpu.VMEM((rows, 1), F32), pltpu.VMEM((rows, LANE), F32)],
        compiler_params=_cparams("parallel"),
        name="moba_prompt",
    )(t5_bias, mqh, mql, mkv, kmean, bias)


def _kmean_kernel(k_ref, o_ref):
    k = k_ref[...]
    nb = k.shape[0] // MB_BLOCK
    o_ref[...] = jnp.sum(k.reshape(nb, MB_BLOCK, LANE), axis=1) * (1.0 / MB_BLOCK)


def _block_means(mk, t_p):
    nb = t_p // MB_BLOCK
    per = _pick_tile(nb, (8,))
    return pl.pallas_call(
        _kmean_kernel,
        grid=(nb // per,),
        in_specs=[pl.BlockSpec((per * MB_BLOCK, LANE), lambda i: (i, 0))],
        out_specs=pl.BlockSpec((per, LANE), lambda i: (i, 0)),
        out_shape=jax.ShapeDtypeStruct((nb, LANE), F32),
        compiler_params=_cparams("parallel"),
        name="moba_kmean",
    )(mk)


def _t5_kernel(t5_ref, bucket_ref, o_ref):
    bucket = bucket_ref[...]
    for h in range(N_HEADS):
        acc = jnp.zeros(bucket.shape, F32)
        for b in range(T5_BUCKETS):
            acc = jnp.where(bucket == b, t5_ref[b, h], acc)
        o_ref[h] = acc


def _t5_tiles(t5_bias, bucket):
    return pl.pallas_call(
        _t5_kernel,
        in_specs=[pl.BlockSpec(memory_space=pltpu.SMEM), pl.BlockSpec(memory_space=pltpu.VMEM)],
        out_specs=pl.BlockSpec(memory_space=pltpu.VMEM),
        out_shape=jax.ShapeDtypeStruct((N_HEADS,) + bucket.shape, F32),
        name="t5_tiles",
    )(t5_bias, bucket)


PAGES_PER_STEP = 8


def _page_specs(cache, layer, n_pages, reverse=False):
    def make(r):
        def index(b, c, pt):
            p = c * PAGES_PER_STEP + r
            if reverse:
                p = n_pages - 1 - p
            return (layer, pt[b * n_pages + p], 0, 0)
        return pl.BlockSpec((None, None) + cache.shape[2:], index)
    return [make(r) for r in range(PAGES_PER_STEP)]


def _feature_major(cache):
    l, pool, page = cache.shape[:3]
    perm = (0, 1) + tuple(range(3, cache.ndim)) + (2,)
    return cache.transpose(perm).reshape(l, pool, -1, page)


def _sample_rows():
    return lax.broadcasted_iota(jnp.int32, (4 * N_HEADS, PAGE_SIZE), 0) // N_HEADS


def _mla_sample_kernel(pt_ref, q_ref, knew_ref, *refs):
    del pt_ref
    ckv_refs = refs[:PAGES_PER_STEP]
    kpe_refs = refs[PAGES_PER_STEP:2 * PAGES_PER_STEP]
    o_ref, m_ref, l_ref, acc_ref = refs[2 * PAGES_PER_STEP:]
    c = pl.program_id(1)
    q = q_ref[...]
    q_lat = q[:, :MLA_KV_LORA]
    q_pe = q[:, MLA_KV_LORA:MLA_KV_LORA + MLA_ROPE]

    @pl.when(c == 0)
    def _():
        _init_softmax(m_ref, l_ref, acc_ref)
        kn = knew_ref[...]
        s = _dot_nt(q, kn) * MLA_SCALE
        col = lax.broadcasted_iota(jnp.int32, s.shape, 1)
        s = jnp.where(col <= _sample_rows(), s, NEG)
        _softmax_update(s, kn[:, :MLA_KV_LORA], m_ref, l_ref, acc_ref)

    for r in range(PAGES_PER_STEP):
        ckv = ckv_refs[r][...].astype(BF16)
        kpe_t = kpe_refs[r][...].astype(BF16)
        s = (_dot_nt(q_lat, ckv) + _dot(q_pe, kpe_t)) * MLA_SCALE
        _softmax_update(s, ckv, m_ref, l_ref, acc_ref)

    @pl.when(c == pl.num_programs(1) - 1)
    def _():
        o_ref[...] = acc_ref[...] / l_ref[...]


def _mla_sample(pt, q, knew, cache_ckv, cache_kpe, layer):
    nrow = 4 * MLA_HEADS
    b = q.shape[0] // nrow
    n_pages = pt.shape[0] // b
    grid_spec = pltpu.PrefetchScalarGridSpec(
        num_scalar_prefetch=1,
        grid=(b, n_pages // PAGES_PER_STEP),
        in_specs=[pl.BlockSpec((nrow, 2 * LANE), lambda i, c, pt: (i, 0)),
                  pl.BlockSpec((None, PAGE_SIZE, 2 * LANE), lambda i, c, pt: (i, 0, 0))]
                 + _page_specs(cache_ckv, layer, n_pages)
                 + _page_specs(cache_kpe, layer, n_pages),
        out_specs=pl.BlockSpec((nrow, MLA_KV_LORA), lambda i, c, pt: (i, 0)),
        scratch_shapes=[pltpu.VMEM((nrow, 1), F32),
                        pltpu.VMEM((nrow, 1), F32), pltpu.VMEM((nrow, MLA_KV_LORA), F32)],
    )
    return pl.pallas_call(
        _mla_sample_kernel,
        grid_spec=grid_spec,
        out_shape=jax.ShapeDtypeStruct((b * nrow, MLA_KV_LORA), F32),
        compiler_params=_cparams("parallel", "arbitrary"),
        name="mla_sample",
    )(pt, q, knew, *([cache_ckv] * PAGES_PER_STEP), *([cache_kpe] * PAGES_PER_STEP))


def _own_group_lanes(x):
    rows = x.shape[0]
    lane = lax.broadcasted_iota(jnp.int32, (rows, LANE), 1)
    g = (lax.broadcasted_iota(jnp.int32, (rows, LANE), 0) % N_HEADS) // (N_HEADS // KV_HEADS)
    return jnp.where((lane >= HEAD_DIM * g) & (lane < HEAD_DIM * (g + 1)), x, 0.0)


def _sb_sample_kernel(pt_ref, q_ref, knew_ref, vnew_ref, uu_ref, *refs):
    del pt_ref
    k_refs = refs[:PAGES_PER_STEP]
    v_refs = refs[PAGES_PER_STEP:2 * PAGES_PER_STEP]
    o_ref, c_ref, acc_ref = refs[2 * PAGES_PER_STEP:]
    c = pl.program_id(1)
    q = q_ref[...]

    def step(k, v, new):
        z = _dot_nt(q, k) if new else _dot(q, k)
        ls, l1m = _log_sigmoid_pair(z)
        if new:
            col = lax.broadcasted_iota(jnp.int32, z.shape, 1)
            mask = col < _sample_rows()
            l1m = jnp.where(mask, l1m, 0.0)
        a = jnp.exp(ls + _suffix_sum(l1m, uu_ref[...]) + c_ref[...])
        if new:
            a = jnp.where(mask, a, 0.0)
        a = a.astype(BF16)
        acc_ref[...] += _dot(a, v) if new else _dot_nt(a, v)
        c_ref[...] += jnp.sum(l1m, axis=-1, keepdims=True)

    @pl.when(c == 0)
    def _():
        c_ref[...] = jnp.zeros(c_ref.shape, F32)
        acc_ref[...] = jnp.zeros(acc_ref.shape, F32)
        step(knew_ref[...], vnew_ref[...], True)

    for r in range(PAGES_PER_STEP):
        step(k_refs[r][...].astype(BF16), v_refs[r][...].astype(BF16), False)

    @pl.when(c == pl.num_programs(1) - 1)
    def _():
        o_ref[...] = _own_group_lanes(acc_ref[...])


def _sb_sample(pt, q, knew, vnew, cache_k, cache_v, layer):
    nrow = 4 * N_HEADS
    b = q.shape[0] // nrow
    n_pages = pt.shape[0] // b
    new_spec = pl.BlockSpec((None, PAGE_SIZE, LANE), lambda i, c, pt: (i, 0, 0))
    grid_spec = pltpu.PrefetchScalarGridSpec(
        num_scalar_prefetch=1,
        grid=(b, n_pages // PAGES_PER_STEP),
        in_specs=[pl.BlockSpec((nrow, LANE), lambda i, c, pt: (i, 0)), new_spec, new_spec,
                  pl.BlockSpec((2 * PAGE_SIZE, PAGE_SIZE), lambda i, c, pt: (0, 0))]
                 + _page_specs(cache_k, layer, n_pages, reverse=True)
                 + _page_specs(cache_v, layer, n_pages, reverse=True),
        out_specs=pl.BlockSpec((nrow, LANE), lambda i, c, pt: (i, 0)),
        scratch_shapes=[pltpu.VMEM((nrow, 1), F32), pltpu.VMEM((nrow, LANE), F32)],
    )
    return pl.pallas_call(
        _sb_sample_kernel,
        grid_spec=grid_spec,
        out_shape=jax.ShapeDtypeStruct((b * nrow, LANE), F32),
        compiler_params=_cparams("parallel", "arbitrary"),
        name="sb_sample",
    )(pt, q, knew, vnew, _suffix_matrix(PAGE_SIZE),
      *([cache_k] * PAGES_PER_STEP), *([cache_v] * PAGES_PER_STEP))


def _moba_score_kernel(pt_ref, qh_ref, ql_ref, *refs):
    del pt_ref
    k_refs = refs[:PAGES_PER_STEP]
    logit_ref, score_ref, kmean_ref = refs[PAGES_PER_STEP:]
    c = pl.program_id(1)
    qh = qh_ref[...]
    pages_per_block = MB_BLOCK // PAGE_SIZE

    @pl.when(c == 0)
    def _():
        kmean_ref[...] = jnp.zeros(kmean_ref.shape, F32)

    blk_lane = lax.broadcasted_iota(jnp.int32, kmean_ref.shape, 1)
    for r in range(0, PAGES_PER_STEP, pages_per_block):
        ksum = jnp.zeros((LANE, 1), F32)
        for rr in range(r, r + pages_per_block):
            k = k_refs[rr][...]
            logit_ref[rr] = _dot(qh, k.astype(BF16))
            ksum = ksum + jnp.sum(k, axis=1, keepdims=True)
        blk = c * (PAGES_PER_STEP // pages_per_block) + r // pages_per_block
        kmean_ref[...] += jnp.where(blk_lane == blk, ksum * (1.0 / MB_BLOCK), 0.0)

    @pl.when(c == pl.num_programs(1) - 1)
    def _():
        kh, kl = _split_bf16(kmean_ref[...])
        score_ref[...] = _dot(qh, kh) + _dot(qh, kl) + _dot(ql_ref[...], kh)


def _moba_scores(pt, qh, ql, cache_k, layer, nbp):
    nrow = 4 * N_HEADS
    b = qh.shape[0] // nrow
    n_pages = pt.shape[0] // b
    qspec = pl.BlockSpec((nrow, LANE), lambda i, c, pt: (i, 0))
    grid_spec = pltpu.PrefetchScalarGridSpec(
        num_scalar_prefetch=1,
        grid=(b, n_pages // PAGES_PER_STEP),
        in_specs=[qspec, qspec] + _page_specs(cache_k, layer, n_pages),
        out_specs=[pl.BlockSpec((None, PAGES_PER_STEP, nrow, PAGE_SIZE), lambda i, c, pt: (i, c, 0, 0)),
                   pl.BlockSpec((nrow, nbp), lambda i, c, pt: (i, 0))],
        scratch_shapes=[pltpu.VMEM((LANE, nbp), F32)],
    )
    return pl.pallas_call(
        _moba_score_kernel,
        grid_spec=grid_spec,
        out_shape=[jax.ShapeDtypeStruct((b, n_pages, nrow, PAGE_SIZE), F32),
                   jax.ShapeDtypeStruct((b * nrow, nbp), F32)],
        compiler_params=_cparams("parallel", "arbitrary"),
        name="moba_sample_scores",
    )(pt, qh, ql, *([cache_k] * PAGES_PER_STEP))


def _moba_value_kernel(pt_ref, t5_ref, qh_ref, knew_ref, vnew_ref, logit_ref, score_ref, bias_ref,
                       *refs, n_blocks):
    del pt_ref
    v_refs = refs[:PAGES_PER_STEP]
    o_ref, p_ref, pnew_ref, acc_ref = refs[PAGES_PER_STEP:]
    c = pl.program_id(1)
    nrow = 4 * N_HEADS
    pages_per_block = MB_BLOCK // PAGE_SIZE

    @pl.when(c == 0)
    def _():
        score = score_ref[...]
        blk = lax.broadcasted_iota(jnp.int32, score.shape, 1)
        sel = _top3_select(score, blk < n_blocks)
        head = lax.broadcasted_iota(jnp.int32, (nrow, 1), 0) % N_HEADS
        far_bias = jnp.zeros((nrow, 1), F32)
        for h in range(N_HEADS):
            far_bias = jnp.where(head == h, t5_ref[T5_BUCKETS - 1, h], far_bias)
        bias = bias_ref[...]
        s_new = _dot_nt(qh_ref[...], knew_ref[...]) + bias[:, MB_BLOCK:MB_BLOCK + PAGE_SIZE]
        col = lax.broadcasted_iota(jnp.int32, s_new.shape, 1)
        s_new = jnp.where(col <= _sample_rows(), s_new, NEG)
        m = jnp.max(s_new, axis=-1, keepdims=True)
        for n in range(n_blocks):
            keep = sel[:, n:n + 1] > 0.0
            for pg in range(n * pages_per_block, (n + 1) * pages_per_block):
                if n == n_blocks - 1:
                    off = (pg - n * pages_per_block) * PAGE_SIZE
                    s = logit_ref[pg] + bias[:, off:off + PAGE_SIZE]
                else:
                    s = logit_ref[pg] + far_bias
                s = jnp.where(keep, s, NEG)
                p_ref[pg] = s
                m = jnp.maximum(m, jnp.max(s, axis=-1, keepdims=True))
        p_new = jnp.exp(s_new - m)
        l = jnp.sum(p_new, axis=-1, keepdims=True)
        for pg in range(n_blocks * pages_per_block):
            p = jnp.exp(p_ref[pg] - m)
            p_ref[pg] = p
            l = l + jnp.sum(p, axis=-1, keepdims=True)
        inv = 1.0 / l
        pnew_ref[...] = p_new * inv
        for pg in range(n_blocks * pages_per_block):
            p_ref[pg] = p_ref[pg] * inv
        acc_ref[...] = _dot((p_new * inv).astype(BF16), vnew_ref[...])

    for r in range(PAGES_PER_STEP):
        p = p_ref[c * PAGES_PER_STEP + r]
        acc_ref[...] += _dot_nt(p.astype(BF16), v_refs[r][...].astype(BF16))

    @pl.when(c == pl.num_programs(1) - 1)
    def _():
        o_ref[...] = _own_group_lanes(acc_ref[...])


def _moba_values(pt, t5_bias, qh, knew, vnew, logits, scores, bias, cache_v, layer):
    nrow = 4 * N_HEADS
    b = qh.shape[0] // nrow
    n_pages = pt.shape[0] // b
    n_blocks = n_pages * PAGE_SIZE // MB_BLOCK
    new_spec = pl.BlockSpec((None, PAGE_SIZE, LANE), lambda i, c, pt: (i, 0, 0))
    grid_spec = pltpu.PrefetchScalarGridSpec(
        num_scalar_prefetch=1,
        grid=(b, n_pages // PAGES_PER_STEP),
        in_specs=[pl.BlockSpec(memory_space=pltpu.SMEM),
                  pl.BlockSpec((nrow, LANE), lambda i, c, pt: (i, 0)), new_spec, new_spec,
                  pl.BlockSpec((None, n_pages, nrow, PAGE_SIZE), lambda i, c, pt: (i, 0, 0, 0)),
                  pl.BlockSpec((nrow, scores.shape[1]), lambda i, c, pt: (i, 0)),
                  pl.BlockSpec(bias.shape, lambda i, c, pt: (0, 0))]
                 + _page_specs(cache_v, layer, n_pages),
        out_specs=pl.BlockSpec((nrow, LANE), lambda i, c, pt: (i, 0)),
        scratch_shapes=[pltpu.VMEM((n_pages, nrow, PAGE_SIZE), F32), pltpu.VMEM((nrow, PAGE_SIZE), F32),
                        pltpu.VMEM((nrow, LANE), F32)],
    )
    return pl.pallas_call(
        functools.partial(_moba_value_kernel, n_blocks=n_blocks),
        grid_spec=grid_spec,
        out_shape=jax.ShapeDtypeStruct((b * nrow, LANE), F32),
        compiler_params=_cparams("parallel", "arbitrary"),
        name="moba_sample_values",
    )(pt, t5_bias, qh, knew, vnew, logits, scores, bias, *([cache_v] * PAGES_PER_STEP))


CONV_HALO = 32


def _conv_prompt_kernel(cur_ref, prev_ref, w_ref, b_ref, g_ref, beta_ref, o_ref, ext_ref, *, tm):
    i = pl.program_id(0)
    prev = prev_ref[...]
    ext_ref[:CONV_HALO, :] = jnp.where(i > 0, prev, 0.0)
    ext_ref[CONV_HALO:, :] = cur_ref[...]
    acc = jnp.zeros((tm, CONV_CH), F32)
    base = CONV_HALO - (CONV_WIDTH - 1)
    for w in range(CONV_WIDTH):
        acc = acc + ext_ref[base + w:base + w + tm, :] * w_ref[w:w + 1, :]
    o_ref[...] = _silu(_layer_norm(acc + b_ref[...], g_ref[...], beta_ref[...]))


def _conv_prompt(glu, t_p, conv_w, conv_b, ln_g, ln_b):
    tm = _pick_tile(t_p, (512, 256, 128, 64, 32))
    per = tm // CONV_HALO
    return pl.pallas_call(
        functools.partial(_conv_prompt_kernel, tm=tm),
        grid=(t_p // tm,),
        in_specs=[pl.BlockSpec((tm, CONV_CH), lambda i: (i, 0)),
                  pl.BlockSpec((CONV_HALO, CONV_CH), lambda i: (jnp.maximum(i * per - 1, 0), 0)),
                  _full(conv_w.shape), _full(conv_b.shape), _full(ln_g.shape), _full(ln_b.shape)],
        out_specs=pl.BlockSpec((tm, CONV_CH), lambda i: (i, 0)),
        out_shape=jax.ShapeDtypeStruct((t_p, CONV_CH), F32),
        scratch_shapes=[pltpu.VMEM((tm + CONV_HALO, CONV_CH), F32)],
        compiler_params=_cparams("parallel"),
        name="conv_prompt",
    )(glu, glu, conv_w, conv_b, ln_g, ln_b)


def _conv_sample_kernel(ext_ref, w_ref, b_ref, g_ref, beta_ref, o_ref):
    n_new = o_ref.shape[0]
    for t in range(n_new):
        acc = jnp.zeros(o_ref.shape[1:], F32)
        for w in range(CONV_WIDTH):
            acc = acc + ext_ref[t + w] * w_ref[w:w + 1, :]
        o_ref[t] = _silu(_layer_norm(acc + b_ref[...], g_ref[...], beta_ref[...]))


def _conv_sample(ext_t, conv_w, conv_b, ln_g, ln_b):
    n_new = ext_t.shape[0] - (CONV_WIDTH - 1)
    vm = pl.BlockSpec(memory_space=pltpu.VMEM)
    return pl.pallas_call(
        _conv_sample_kernel,
        in_specs=[vm] * 5,
        out_specs=vm,
        out_shape=jax.ShapeDtypeStruct((n_new,) + ext_t.shape[1:], F32),
        name="conv_sample",
    )(ext_t, conv_w, conv_b, ln_g, ln_b)


def _merge_kernel(x_ref, cact_ref, lat_ref, sb_ref, mb_ref, gate_ref, gb_ref, wc_ref, wuv_ref, wm_ref,
                  ws_ref, wo_ref, wout_ref, g_ref, b_ref, o_ref, *, alpha):
    c = _dot(cact_ref[...].astype(BF16), wc_ref[...])
    mv = _dot(lat_ref[...].astype(BF16), wuv_ref[...])
    m = _dot(mv.astype(BF16), wm_ref[...])
    s = _dot(sb_ref[...].astype(BF16), ws_ref[...])
    o = _dot(mb_ref[...].astype(BF16), wo_ref[...])
    g = jax.nn.sigmoid(gate_ref[...] + gb_ref[...])
    d = D_MODEL
    y = g[:, :d] * c + g[:, d:2 * d] * m + g[:, 2 * d:3 * d] * s + g[:, 3 * d:] * o
    y = _dot(y.astype(BF16), wout_ref[...])
    o_ref[...] = _layer_norm(alpha * x_ref[...] + y, g_ref[...], b_ref[...])


def _merge(x, cact, lat, sb, mb, gate, gate_b, wc, wuv, wm, ws, wo, wout, ln_g, ln_b, alpha):
    n = x.shape[0]
    tm = _pick_tile(n, (256, 128, 64, 32, 16, 8))
    row = lambda w: pl.BlockSpec((tm, w), lambda i: (i, 0))
    weights = (gate_b, wc, wuv, wm, ws, wo, wout, ln_g, ln_b)
    return pl.pallas_call(
        functools.partial(_merge_kernel, alpha=alpha),
        grid=(n // tm,),
        in_specs=[row(a.shape[1]) for a in (x, cact, lat, sb, mb, gate)] + [_full(w.shape) for w in weights],
        out_specs=row(D_MODEL),
        out_shape=jax.ShapeDtypeStruct((n, D_MODEL), F32),
        compiler_params=_cparams("parallel"),
        name="merge",
    )(x, cact, lat, sb, mb, gate, *weights)


def _first_index_of_max(s, idx):
    mx = jnp.max(s, axis=0, keepdims=True)
    first = jnp.min(jnp.where(s == mx, idx, 1e9), axis=0, keepdims=True)
    return idx == first


def _router_kernel(x_ref, wh_ref, wl_ref, bias_ref, gate_ref, xb_ref):
    x = x_ref[...]
    xh, xl = _split_bf16(x)
    xb_ref[...] = xh
    logits = (_dot_nt(wh_ref[...], xh) + _dot_nt(wl_ref[...], xh) + _dot_nt(wh_ref[...], xl))[:N_EXPERTS]
    scores = jax.nn.sigmoid(logits)
    biased = scores + bias_ref[...]
    e, tm = biased.shape
    per = e // N_GROUPS
    grp = biased.reshape(N_GROUPS, per, tm)
    sub = lax.broadcasted_iota(jnp.int32, grp.shape, 1).astype(F32)
    top1 = jnp.max(grp, axis=1, keepdims=True)
    first = jnp.min(jnp.where(grp == top1, sub, 1e9), axis=1, keepdims=True)
    top2 = jnp.max(jnp.where(sub == first, -jnp.inf, grp), axis=1, keepdims=True)
    gs = jnp.broadcast_to(top1 + top2, grp.shape).reshape(e, tm)
    eidx = lax.broadcasted_iota(jnp.int32, biased.shape, 0)
    gidx = (eidx // per).astype(F32)
    e_keep = jnp.zeros(biased.shape, F32)
    for _ in range(TOPK_GROUPS):
        pick = _first_index_of_max(gs, gidx)
        e_keep = jnp.where(pick, 1.0, e_keep)
        gs = jnp.where(pick, -jnp.inf, gs)
    eidx = eidx.astype(F32)
    cand = jnp.where(e_keep > 0.0, biased, -jnp.inf)
    chosen = jnp.zeros(biased.shape, F32)
    for _ in range(TOP_K):
        pick = _first_index_of_max(cand, eidx)
        chosen = jnp.where(pick, 1.0, chosen)
        cand = jnp.where(pick, -jnp.inf, cand)
    sel = chosen * scores
    wts = sel / jnp.sum(sel, axis=0, keepdims=True) * ROUTED_SCALE
    gate_ref[...] = jnp.concatenate([wts, jnp.zeros_like(wts)], axis=0).T


def _router(x, router_w, router_bias):
    n = x.shape[0]
    tm = _pick_tile(n, (512, 256, 128))
    wt = jnp.pad(router_w.T, ((0, LANE - N_EXPERTS), (0, 0)))
    wh, wl = _split_bf16(wt)
    return pl.pallas_call(
        _router_kernel,
        grid=(n // tm,),
        in_specs=[pl.BlockSpec((tm, D_MODEL), lambda i: (i, 0)), _full(wh.shape), _full(wl.shape),
                  _full((N_EXPERTS, 1))],
        out_specs=[pl.BlockSpec((tm, LANE), lambda i: (i, 0)),
                   pl.BlockSpec((tm, D_MODEL), lambda i: (i, 0))],
        out_shape=[jax.ShapeDtypeStruct((n, LANE), F32), jax.ShapeDtypeStruct((n, D_MODEL), BF16)],
        compiler_params=_cparams("parallel"),
        name="moe_router",
    )(x, wh, wl, router_bias.reshape(N_EXPERTS, 1))


def _experts_kernel(x_ref, xb_ref, gate_ref, wg_ref, wu_ref, wd_ref, wsg_ref, wsu_ref, wsd_ref,
                    g_ref, b_ref, o_ref, acc_ref, *, alpha):
    e = pl.program_id(1)
    xb = xb_ref[...]

    @pl.when(e == 0)
    def _():
        h = _silu(_dot(xb, wsg_ref[...])) * _dot(xb, wsu_ref[...])
        acc_ref[...] = _dot(h.astype(BF16), wsd_ref[...])

    w = _pick_column(gate_ref[...], e)
    h = _silu(_dot(xb, wg_ref[...])) * _dot(xb, wu_ref[...]) * w
    acc_ref[...] += _dot(h.astype(BF16), wd_ref[...])

    @pl.when(e == pl.num_programs(1) - 1)
    def _():
        o_ref[...] = _layer_norm(alpha * x_ref[...] + acc_ref[...], g_ref[...], b_ref[...])


def _experts(x, xb, gate, wg, wu, wd, wsg, wsu, wsd, ln_g, ln_b, alpha):
    n = x.shape[0]
    tm = _pick_tile(n, (1024, 512, 256, 128))
    row = lambda w: pl.BlockSpec((tm, w), lambda i, e: (i, 0))
    return pl.pallas_call(
        functools.partial(_experts_kernel, alpha=alpha),
        grid=(n // tm, N_EXPERTS),
        in_specs=[row(D_MODEL), row(D_MODEL), row(LANE),
                  pl.BlockSpec((None, D_MODEL, EXPERT_FF), lambda i, e: (e, 0, 0)),
                  pl.BlockSpec((None, D_MODEL, EXPERT_FF), lambda i, e: (e, 0, 0)),
                  pl.BlockSpec((None, EXPERT_FF, D_MODEL), lambda i, e: (e, 0, 0)),
                  _full(wsg.shape), _full(wsu.shape), _full(wsd.shape), _full(ln_g.shape), _full(ln_b.shape)],
        out_specs=row(D_MODEL),
        out_shape=jax.ShapeDtypeStruct((n, D_MODEL), F32),
        scratch_shapes=[pltpu.VMEM((tm, D_MODEL), F32)],
        compiler_params=_cparams("parallel", "arbitrary"),
        name="moe_experts",
    )(x, xb, gate, wg, wu, wd, wsg, wsu, wsd, ln_g, ln_b)


def _vec(v):
    return v.reshape(1, -1)


def _pad_new(a):
    return jnp.pad(a, ((0, 0), (0, PAGE_SIZE - a.shape[1]), (0, 0))).astype(BF16)


def kernel(x_prompt, x_sample, cache_mla_ckv, cache_mla_kpe, cache_sb_k, cache_sb_v, cache_moba_k, cache_moba_v, state_conv, page_table, t5_bias, w_in, gate_b, conv_w, conv_b, conv_ln_g, conv_ln_b, w_conv_out, mla_q_norm, w_q_up, mla_kv_norm, w_kv_up, w_mla_out, w_sb_out, w_mb_out, w_out, ln1_g, ln1_b, router_w, router_bias, w_e_gate, w_e_up, w_e_down, w_s_gate, w_s_up, w_s_down, ln2_g, ln2_b):
    depth = w_in.shape[0]
    alpha = (2 * depth) ** 0.25
    bp, t_p, _ = x_prompt.shape
    assert bp == 1
    bs, n_new, _ = x_sample.shape
    n_s = bs * n_new
    n_pages = page_table.shape[1]
    past = n_pages * PAGE_SIZE
    nrow = n_new * N_HEADS
    assert n_new == 4 and t_p % MB_BLOCK == 0 and past % MB_BLOCK == 0 and n_pages % PAGES_PER_STEP == 0

    x = jnp.concatenate([x_prompt.reshape(t_p, D_MODEL), x_sample.reshape(n_s, D_MODEL)], axis=0)
    pos = jnp.concatenate([jnp.arange(t_p, dtype=jnp.int32),
                           jnp.tile(past + jnp.arange(n_new, dtype=jnp.int32), bs)])
    cos, sin = _rope_tables(pos)
    pt = page_table.reshape(-1)
    kpe_t, sbk_t, sbv_t, mbk_t, mbv_t = (
        _feature_major(c) for c in (cache_mla_kpe, cache_sb_k, cache_sb_v, cache_moba_k, cache_moba_v))

    tq = MB_BLOCK
    rel_p = jnp.arange(tq)[:, None] - jnp.arange(2 * tq)[None, :] + tq
    bias_p = _t5_tiles(t5_bias, _t5_bucket(rel_p).astype(jnp.int32)).reshape(N_HEADS * tq, 2 * tq)
    kpos_s = jnp.concatenate([past - MB_BLOCK + jnp.arange(MB_BLOCK), past + jnp.arange(PAGE_SIZE)])
    rel_s = past + jnp.arange(8)[:, None] - kpos_s[None, :]
    bias_s = _t5_tiles(t5_bias, _t5_bucket(rel_s).astype(jnp.int32))
    bias_s = bias_s[:, :n_new].transpose(1, 0, 2).reshape(nrow, MB_BLOCK + PAGE_SIZE)

    nb_p = t_p // MB_BLOCK
    nb_s = past // MB_BLOCK
    nbp_p = -(-nb_p // LANE) * LANE
    nbp_s = -(-nb_s // LANE) * LANE

    new_p = {k: [] for k in ("ckv", "kpe", "sk", "sv", "mk", "mv", "conv")}
    new_s = {k: [] for k in new_p}
    for l in range(depth):
        w1, wg, wq, wuk, wuv = _pack_layer_weights(w_in[l], w_q_up[l], w_kv_up[l])
        pc = _project(x, cos, sin, w1, wg, wq, wuk, _vec(mla_q_norm[l]), _vec(mla_kv_norm[l]))
        for k in ("ckv", "kpe", "sk", "sv", "mk", "mv"):
            new_p[k].append(pc[k][:t_p])
            new_s[k].append(pc[k][t_p:])

        lat_p = _mla_prompt(pc["qcat"], pc["kcat"], t_p, _pick_tile(t_p, (256, 128)))
        sb_p = _sb_prompt(pc["sq"], pc["skv"], t_p, _pick_tile(t_p, (256, 128)))
        kmean = jnp.pad(_block_means(pc["mk"], t_p), ((0, nbp_p - nb_p), (0, 0)))
        mb_p = _moba_prompt(t5_bias, pc["mqh"], pc["mql"], pc["mkv"], kmean, bias_p, t_p)
        glu_p = pc["glu"][:t_p]
        cact_p = _conv_prompt(pc["glu"], t_p, conv_w[l], _vec(conv_b[l]), _vec(conv_ln_g[l]), _vec(conv_ln_b[l]))
        new_p["conv"].append(glu_p[t_p - (CONV_WIDTH - 1):])

        def rows(a, width):
            return a[t_p:].reshape(n_s * N_HEADS, width)

        def newkeys(a):
            return _pad_new(a[t_p:].reshape(bs, n_new, a.shape[1]))

        lat_s = _mla_sample(pt, rows(pc["qcat"], 2 * LANE), newkeys(pc["kcat"]),
                            cache_mla_ckv, kpe_t, l)
        skv_new = newkeys(pc["skv"])
        sb_s = _sb_sample(pt, rows(pc["sq"], LANE), skv_new[..., :LANE], skv_new[..., LANE:], sbk_t, sbv_t, l)
        mkv_new = newkeys(pc["mkv"])
        mqh_s = rows(pc["mqh"], LANE)
        logits, scores = _moba_scores(pt, mqh_s, rows(pc["mql"], LANE), mbk_t, l, nbp_s)
        mb_s = _moba_values(pt, t5_bias, mqh_s, mkv_new[..., :LANE], mkv_new[..., LANE:], logits, scores,
                            bias_s, mbv_t, l)
        ext_t = jnp.concatenate([state_conv[l].transpose(1, 0, 2),
                                 pc["glu"][t_p:].reshape(bs, n_new, CONV_CH).transpose(1, 0, 2)], axis=0)
        new_s["conv"].append(ext_t[n_new:].transpose(1, 0, 2))
        cact_s = _conv_sample(ext_t, conv_w[l], _vec(conv_b[l]), _vec(conv_ln_g[l]), _vec(conv_ln_b[l]))
        cact_s = cact_s.transpose(1, 0, 2).reshape(n_s, CONV_CH)

        cact = jnp.concatenate([cact_p, cact_s], axis=0)
        lat = jnp.concatenate([lat_p, lat_s.reshape(n_s, MLA_HEADS * MLA_KV_LORA)], axis=0)
        sb = jnp.concatenate([sb_p, sb_s.reshape(n_s, N_HEADS * LANE)], axis=0)
        mb = jnp.concatenate([mb_p, mb_s.reshape(n_s, N_HEADS * LANE)], axis=0)
        x1 = _merge(x, cact, lat, sb, mb, pc["gate"], _vec(gate_b[l]),
                    w_conv_out[l].astype(BF16), wuv, w_mla_out[l].astype(BF16),
                    _head_pad_cols(w_sb_out[l].T).T.astype(BF16), _head_pad_cols(w_mb_out[l].T).T.astype(BF16),
                    w_out[l].astype(BF16), _vec(ln1_g[l]), _vec(ln1_b[l]), alpha)
        gate, xb = _router(x1, router_w[l], router_bias[l])
        x = _experts(x1, xb, gate, w_e_gate[l].astype(BF16), w_e_up[l].astype(BF16), w_e_down[l].astype(BF16),
                     w_s_gate[l].astype(BF16), w_s_up[l].astype(BF16), w_s_down[l].astype(BF16),
                     _vec(ln2_g[l]), _vec(ln2_b[l]), alpha)

    def stack_p(k, tail):
        return jnp.stack(new_p[k]).reshape((depth, 1, t_p) + tail)

    def stack_s(k, tail):
        return jnp.stack(new_s[k]).reshape((depth, bs, n_new) + tail)

    kv = (KV_HEADS, HEAD_DIM)
    return (x[:t_p].reshape(1, t_p, D_MODEL), x[t_p:].reshape(bs, n_new, D_MODEL),
            stack_p("ckv", (MLA_KV_LORA,)), stack_s("ckv", (MLA_KV_LORA,)),
            stack_p("kpe", (MLA_ROPE,)), stack_s("kpe", (MLA_ROPE,)),
            stack_p("sk", kv), stack_s("sk", kv), stack_p("sv", kv), stack_s("sv", kv),
            stack_p("mk", kv), stack_s("mk", kv), stack_p("mv", kv), stack_s("mv", kv),
            jnp.stack(new_p["conv"]).reshape(depth, 1, CONV_WIDTH - 1, CONV_CH),
            jnp.stack(new_s["conv"]))
```

```python
import functools
import math

import numpy as np
import jax
import jax.numpy as jnp
from jax import lax
from jax.experimental import pallas as pl
from jax.experimental.pallas import tpu as pltpu

F32 = jnp.float32
BF16 = jnp.bfloat16

D_MODEL = 1024
HEAD_DIM = 64
CONV_CH = 256
CONV_WIDTH = 31
MLA_HEADS = 4
MLA_Q_LORA = 256
MLA_KV_LORA = 128
MLA_NOPE = 64
MLA_ROPE = 32
MLA_V = 64
MLA_SCALE = (MLA_NOPE + MLA_ROPE) ** -0.5
ROPE_THETA = 10000.0
N_HEADS = 4
KV_HEADS = 2
QK_SCALE = HEAD_DIM ** -0.5
MB_BLOCK = 256
MB_TOPK = 3
T5_BUCKETS = 32
T5_MAX_DIST = 128
N_BRANCH = 4
N_EXPERTS = 64
EXPERT_FF = 256
SHARED_FF = 256
TOP_K = 8
N_GROUPS = 8
TOPK_GROUPS = 4
ROUTED_SCALE = 2.5
LN_EPS = 1e-5
PAGE_SIZE = 128

LANE = 128
HALF_LANE = LANE // 2
NEG = -1e30
LOG2_E = math.log2(math.e)
VMEM_LIMIT = 56 * 1024 * 1024

NT_DIMS = (((1,), (1,)), ((), ()))


def _cparams(*sem):
    return pltpu.CompilerParams(dimension_semantics=sem, vmem_limit_bytes=VMEM_LIMIT)


def _pick_tile(n, cands):
    for c in cands:
        if n % c == 0:
            return c
    raise ValueError(f"no tile for {n}")


def _full(shape):
    nd = len(shape)
    return pl.BlockSpec(shape, lambda *_: (0,) * nd)


def _dot(a, b):
    return jnp.dot(a, b, preferred_element_type=F32)


def _dot_nt(a, b):
    return lax.dot_general(a, b, NT_DIMS, preferred_element_type=F32)


def _split_bf16(x):
    hi = x.astype(BF16)
    lo = (x - hi.astype(F32)).astype(BF16)
    return hi, lo


def _layer_norm(x, g, b):
    mu = jnp.mean(x, axis=-1, keepdims=True)
    d = x - mu
    var = jnp.mean(d * d, axis=-1, keepdims=True)
    return d * lax.rsqrt(var + LN_EPS) * g + b


def _rms_norm(x, g):
    return x * lax.rsqrt(jnp.mean(x * x, axis=-1, keepdims=True) + LN_EPS) * g


def _silu(x):
    return x * jax.nn.sigmoid(x)


def _head_pad_cols(w):
    k = w.shape[0]
    wh = w.reshape(k, N_HEADS, HEAD_DIM)
    z = jnp.zeros_like(wh)
    per_head = [jnp.concatenate([wh[:, h], z[:, h]] if h // 2 == 0 else [z[:, h], wh[:, h]], axis=-1)
                for h in range(N_HEADS)]
    return jnp.concatenate(per_head, axis=-1)


def _rot_cols(w):
    half = w.shape[-1] // 2
    return jnp.concatenate([-w[..., half:], w[..., :half]], axis=-1)


def _pad_cols(w, width):
    return jnp.pad(w, ((0, 0), (0, width - w.shape[-1])))


C_CONV, C_CQ, C_CKV, C_SQ, C_SK, C_SV, C_MQ, C_MK, C_MV, C_KPE, C_KROT, C_END = (
    0, 512, 768, 896, 1408, 1536, 1664, 2176, 2304, 2432, 2560, 2688)


def _pack_layer_weights(w_in, w_q_up, w_kv_up):
    pts = np.cumsum((2 * CONV_CH, MLA_Q_LORA, MLA_KV_LORA, MLA_ROPE, 256, 128, 128, 256, 128, 128))
    cv, cq, ckv, kpe, sq, sk, sv, mq, mk, mv, gate = jnp.split(w_in, pts.tolist(), axis=-1)
    w1 = jnp.concatenate([
        cv, cq, ckv, _head_pad_cols(sq), sk, sv, _head_pad_cols(mq), mk, mv,
        _pad_cols(kpe, LANE), _pad_cols(_rot_cols(kpe), LANE)], axis=-1).astype(BF16)
    wq = w_q_up.reshape(MLA_Q_LORA, MLA_HEADS, MLA_NOPE + MLA_ROPE)
    nope = wq[:, :, :MLA_NOPE].reshape(MLA_Q_LORA, MLA_HEADS * MLA_NOPE)
    pe = [_pad_cols(wq[:, h, MLA_NOPE:], LANE) for h in range(MLA_HEADS)]
    rot = [_pad_cols(_rot_cols(wq[:, h, MLA_NOPE:]), LANE) for h in range(MLA_HEADS)]
    wq_p = jnp.concatenate([nope] + pe + rot, axis=-1).astype(BF16)
    wkv = w_kv_up.reshape(MLA_KV_LORA, MLA_HEADS, MLA_NOPE + MLA_V)
    wuk = jnp.zeros((MLA_HEADS * MLA_NOPE, MLA_HEADS * MLA_KV_LORA), F32)
    wuv = jnp.zeros((MLA_HEADS * MLA_KV_LORA, MLA_HEADS * MLA_V), F32)
    for h in range(MLA_HEADS):
        wuk = wuk.at[h * MLA_NOPE:(h + 1) * MLA_NOPE, h * MLA_KV_LORA:(h + 1) * MLA_KV_LORA].set(
            wkv[:, h, :MLA_NOPE].T)
        wuv = wuv.at[h * MLA_KV_LORA:(h + 1) * MLA_KV_LORA, h * MLA_V:(h + 1) * MLA_V].set(
            wkv[:, h, MLA_NOPE:])
    return w1, gate.astype(BF16), wq_p, wuk.astype(BF16), wuv.astype(BF16)


def _rope_tables(pos):
    half = MLA_ROPE // 2
    inv = ROPE_THETA ** (-jnp.arange(half, dtype=F32) / half)
    ang = pos.astype(F32)[:, None] * inv[None, :]
    cos = jnp.tile(jnp.cos(ang), (1, 2 * MLA_HEADS))
    sin = jnp.tile(jnp.sin(ang), (1, 2 * MLA_HEADS))
    return cos, sin


def _t5_bucket(rel):
    n = jnp.maximum(rel, 0)
    exact = T5_BUCKETS // 2
    nf = jnp.maximum(n, 1).astype(F32)
    large = exact + (jnp.log(nf / exact) / math.log(T5_MAX_DIST / exact)
                     * (T5_BUCKETS - exact)).astype(jnp.int32)
    large = jnp.minimum(large, T5_BUCKETS - 1)
    return jnp.where(n < exact, n, large)


def _proj_kernel(x_ref, cos_ref, sin_ref, w1_ref, wg_ref, wq_ref, wuk_ref, qn_ref, kvn_ref,
                 glu_ref, ckv_ref, kpe_ref, kcat_ref, qcat_ref, sq_ref, sk_ref, sv_ref, skv_ref,
                 mqh_ref, mql_ref, mk_ref, mv_ref, mkx_ref, mvb_ref, gate_ref, *, tm, t_p):
    x = x_ref[...].astype(BF16)
    u = _dot(x, w1_ref[...])
    gate_ref[...] = _dot(x, wg_ref[...])
    cos = cos_ref[...]
    sin = sin_ref[...]
    glu_ref[...] = u[:, C_CONV:C_CONV + CONV_CH] * jax.nn.sigmoid(u[:, C_CONV + CONV_CH:C_CQ])
    ckvn = _rms_norm(u[:, C_CKV:C_SQ], kvn_ref[...])
    ckv_ref[...] = ckvn
    kpe = u[:, C_KPE:C_KROT] * cos + u[:, C_KROT:C_END] * sin
    kpe_ref[...] = kpe[:, :MLA_ROPE]
    kcat_ref[...] = jnp.concatenate([ckvn, kpe], axis=-1).astype(BF16)
    cqn = _rms_norm(u[:, C_CQ:C_CKV], qn_ref[...])
    qa = _dot(cqn.astype(BF16), wq_ref[...])
    nq = MLA_HEADS * MLA_NOPE
    qlat = _dot(qa[:, :nq].astype(BF16), wuk_ref[...])
    parts = []
    for h in range(MLA_HEADS):
        pe = (qa[:, nq + LANE * h:nq + LANE * (h + 1)] * cos
              + qa[:, nq + LANE * (MLA_HEADS + h):nq + LANE * (MLA_HEADS + h + 1)] * sin)
        parts += [qlat[:, MLA_KV_LORA * h:MLA_KV_LORA * (h + 1)], pe]
    qcat_ref[...] = jnp.concatenate(parts, axis=-1).astype(BF16)
    sq_ref[...] = (u[:, C_SQ:C_SK] * QK_SCALE).astype(BF16)
    sk = u[:, C_SK:C_SV]
    sv = u[:, C_SV:C_MQ]
    sk_ref[...] = sk
    sv_ref[...] = sv
    skv_ref[...] = jnp.concatenate([sk, sv], axis=-1).astype(BF16)
    mqh, mql = _split_bf16(u[:, C_MQ:C_MK] * QK_SCALE)
    mqh_ref[...] = mqh
    mql_ref[...] = mql
    mk = u[:, C_MK:C_MV]
    mv = u[:, C_MV:C_KPE]
    mk_ref[...] = mk
    mv_ref[...] = mv
    mvb_ref[...] = mv.astype(BF16)
    row = pl.program_id(0) * tm + lax.broadcasted_iota(jnp.int32, (tm, LANE), 0)
    lane = lax.broadcasted_iota(jnp.int32, (tm, LANE), 1)
    blk = row // MB_BLOCK
    onehot = jnp.where(lane == blk, 1.0, jnp.where(lane == blk + HALF_LANE, 1.0, 0.0))
    onehot = jnp.where(row < t_p, onehot, 0.0)
    mkx_ref[...] = jnp.concatenate([mk, onehot], axis=-1).astype(BF16)


def _project(x, cos, sin, w1, wg, wq, wuk, qn, kvn, t_p):
    n = x.shape[0]
    tm = _pick_tile(n, (256, 128, 64, 32, 16, 8))
    row = lambda w: pl.BlockSpec((tm, w), lambda i: (i, 0))
    outs = [("glu", CONV_CH, F32), ("ckv", MLA_KV_LORA, F32), ("kpe", MLA_ROPE, F32),
            ("kcat", 2 * LANE, BF16), ("qcat", MLA_HEADS * 2 * LANE, BF16),
            ("sq", N_HEADS * LANE, BF16), ("sk", LANE, F32), ("sv", LANE, F32), ("skv", 2 * LANE, BF16),
            ("mqh", N_HEADS * LANE, BF16), ("mql", N_HEADS * LANE, BF16),
            ("mk", LANE, F32), ("mv", LANE, F32), ("mkx", 2 * LANE, BF16), ("mvb", LANE, BF16),
            ("gate", N_BRANCH * D_MODEL, F32)]
    res = pl.pallas_call(
        functools.partial(_proj_kernel, tm=tm, t_p=t_p),
        grid=(n // tm,),
        in_specs=[row(D_MODEL), row(LANE), row(LANE), _full(w1.shape), _full(wg.shape), _full(wq.shape),
                  _full(wuk.shape), _full(qn.shape), _full(kvn.shape)],
        out_specs=[row(w) for _, w, _ in outs],
        out_shape=[jax.ShapeDtypeStruct((n, w), dt) for _, w, dt in outs],
        compiler_params=_cparams("parallel"),
        name="proj",
    )(x, cos, sin, w1, wg, wq, wuk, qn, kvn)
    return {name: r for (name, _, _), r in zip(outs, res)}


def _causal_ids(tq):
    row = lax.broadcasted_iota(jnp.int32, (tq, tq), 0)
    col = lax.broadcasted_iota(jnp.int32, (tq, tq), 1)
    return row, col


def _flash_init(m_ref, acc_ref):
    m_ref[...] = jnp.full(m_ref.shape, NEG, F32)
    acc_ref[...] = jnp.zeros(acc_ref.shape, F32)


def _flash_update(h, s, v_ones, m_ref, acc_ref, base2=False):
    ex = jnp.exp2 if base2 else jnp.exp
    m_prev = m_ref[h]
    m_new = jnp.maximum(m_prev, jnp.max(s, axis=-1, keepdims=True))
    p = ex(s - jnp.tile(m_new, (1, s.shape[1] // LANE)))
    alpha = ex(m_prev - m_new)
    acc_ref[h] = jnp.tile(alpha, (1, 2)) * acc_ref[h] + _dot(p.astype(BF16), v_ones)
    m_ref[h] = m_new


def _flash_result(h, acc_ref):
    acc = acc_ref[h]
    return acc[:, :LANE] / acc[:, LANE:]


def _mla_prompt_kernel(q_ref, k_ref, o_ref, m_ref, acc_ref, *, tq):
    i = pl.program_id(0)
    _flash_init(m_ref, acc_ref)
    ones = jnp.ones((tq, LANE), BF16)

    def step(j, masked):
        k = k_ref[pl.ds(pl.multiple_of(j * tq, tq), tq), :]
        v_ones = jnp.concatenate([k[:, :MLA_KV_LORA], ones], axis=-1)
        for h in range(MLA_HEADS):
            s = _dot_nt(q_ref[:, h * 2 * LANE:(h + 1) * 2 * LANE], k) * (MLA_SCALE * LOG2_E)
            if masked:
                row, col = _causal_ids(tq)
                s = jnp.where(col <= row, s, NEG)
            _flash_update(h, s, v_ones, m_ref, acc_ref, base2=True)

    def body(t, carry):
        step(2 * t, False)
        step(2 * t + 1, False)
        return carry

    lax.fori_loop(0, i // 2, body, 0)

    @pl.when(i % 2 == 1)
    def _():
        step(i - 1, False)

    step(i, True)
    for h in range(MLA_HEADS):
        o_ref[:, h * MLA_KV_LORA:(h + 1) * MLA_KV_LORA] = _flash_result(h, acc_ref)


def _mla_prompt(qcat, kcat, t_p, tq):
    return pl.pallas_call(
        functools.partial(_mla_prompt_kernel, tq=tq),
        grid=(t_p // tq,),
        in_specs=[pl.BlockSpec((tq, MLA_HEADS * 2 * LANE), lambda i: (i, 0)),
                  pl.BlockSpec((t_p, 2 * LANE), lambda i: (0, 0))],
        out_specs=pl.BlockSpec((tq, MLA_HEADS * MLA_KV_LORA), lambda i: (i, 0)),
        out_shape=jax.ShapeDtypeStruct((t_p, MLA_HEADS * MLA_KV_LORA), F32),
        scratch_shapes=[pltpu.VMEM((MLA_HEADS, tq, LANE), F32), pltpu.VMEM((MLA_HEADS, tq, 2 * LANE), F32)],
        compiler_params=_cparams("parallel"),
        name="mla_prompt",
    )(qcat, kcat)


def _log_sigmoid_pair(z):
    ls = jnp.minimum(z, 0.0) - jnp.log(1.0 + jnp.exp(-jnp.abs(z)))
    return ls, ls - z


def _suffix_sum(l1m, uu):
    hi, lo = _split_bf16(l1m)
    return _dot(jnp.concatenate([hi, lo], axis=-1), uu)


def _group_lanes(x, h):
    lane = lax.broadcasted_iota(jnp.int32, x.shape, 1)
    g = h // (N_HEADS // KV_HEADS)
    return jnp.where((lane >= HEAD_DIM * g) & (lane < HEAD_DIM * (g + 1)), x, 0.0)


def _sb_prompt_kernel(q_ref, kv_ref, uu_ref, o_ref, qs_ref, c_ref, acc_ref, *, tq):
    i = pl.program_id(0)
    rows = N_HEADS * tq
    for h in range(N_HEADS):
        qs_ref[h * tq:(h + 1) * tq, :] = q_ref[:, h * LANE:(h + 1) * LANE]
    c_ref[...] = jnp.zeros(c_ref.shape, F32)
    acc_ref[...] = jnp.zeros(acc_ref.shape, F32)

    def step(j, masked):
        kv = kv_ref[pl.ds(pl.multiple_of(j * tq, tq), tq), :]
        z = _dot_nt(qs_ref[...], kv[:, :LANE])
        ls, l1m = _log_sigmoid_pair(z)
        if masked:
            row = lax.broadcasted_iota(jnp.int32, (rows, tq), 0) & (tq - 1)
            col = lax.broadcasted_iota(jnp.int32, (rows, tq), 1)
            mask = col < row
            l1m = jnp.where(mask, l1m, 0.0)
        a = jnp.exp(ls + _suffix_sum(l1m, uu_ref[...]) + c_ref[...])
        if masked:
            a = jnp.where(mask, a, 0.0)
        acc_ref[...] += _dot(a.astype(BF16), kv[:, LANE:])
        c_ref[...] += jnp.sum(l1m, axis=-1, keepdims=True)

    step(i, True)

    def body(t, carry):
        step(i - 1 - 2 * t, False)
        step(i - 2 - 2 * t, False)
        return carry

    lax.fori_loop(0, i // 2, body, 0)

    @pl.when(i % 2 == 1)
    def _():
        step(0, False)

    acc = acc_ref[...]
    for h in range(N_HEADS):
        o_ref[:, h * LANE:(h + 1) * LANE] = _group_lanes(acc[h * tq:(h + 1) * tq], h)


def _suffix_matrix(tk):
    u = (np.arange(tk)[:, None] > np.arange(tk)[None, :]).astype(np.float32)
    return jnp.asarray(np.concatenate([u, u], axis=0), dtype=BF16)


def _sb_prompt(sq, skv, t_p, tq):
    return pl.pallas_call(
        functools.partial(_sb_prompt_kernel, tq=tq),
        grid=(t_p // tq,),
        in_specs=[pl.BlockSpec((tq, N_HEADS * LANE), lambda i: (i, 0)),
                  pl.BlockSpec((t_p, 2 * LANE), lambda i: (0, 0)),
                  _full((2 * tq, tq))],
        out_specs=pl.BlockSpec((tq, N_HEADS * LANE), lambda i: (i, 0)),
        out_shape=jax.ShapeDtypeStruct((t_p, N_HEADS * LANE), F32),
        scratch_shapes=[pltpu.VMEM((N_HEADS * tq, LANE), BF16), pltpu.VMEM((N_HEADS * tq, 1), F32),
                        pltpu.VMEM((N_HEADS * tq, LANE), F32)],
        compiler_params=_cparams("parallel"),
        name="sb_prompt",
    )(sq, skv, _suffix_matrix(tq))


def _top3_select(score, valid):
    lane = lax.broadcasted_iota(jnp.int32, score.shape, 1).astype(F32)
    s = jnp.where(valid, score, -jnp.inf)
    sel = jnp.zeros(score.shape, F32)
    for _ in range(MB_TOPK):
        mx = jnp.max(s, axis=-1, keepdims=True)
        cand = jnp.where(s == mx, lane, 1e9)
        cand = jnp.where(mx > -jnp.inf, cand, 1e9)
        pick = lane == jnp.min(cand, axis=-1, keepdims=True)
        sel = jnp.where(pick, 1.0, sel)
        s = jnp.where(pick, -jnp.inf, s)
    return sel


def _block_scores(qh, ql, kmean):
    kh, kl = _split_bf16(kmean)
    return _dot_nt(qh, kh) + _dot_nt(qh, kl) + _dot_nt(ql, kh)


def _moba_prompt_kernel(t5_ref, qh_ref, ql_ref, kx_ref, v_ref, kmean_ref, bias_ref, o_ref,
                        qx_ref, m_ref, acc_ref, *, tq):
    i = pl.program_id(0)
    _flash_init(m_ref, acc_ref)
    ones = jnp.ones((tq, LANE), BF16)
    lane = lax.broadcasted_iota(jnp.int32, (tq, LANE), 1)
    kmean = kmean_ref[...]
    for h in range(N_HEADS):
        qh = qh_ref[:, h * LANE:(h + 1) * LANE]
        sel = _top3_select(_block_scores(qh, ql_ref[:, h * LANE:(h + 1) * LANE], kmean), lane < i)
        far_hi, far_lo = _split_bf16(jnp.full((tq, LANE), t5_ref[T5_BUCKETS - 1, h], F32))
        hi = jnp.where(sel > 0.0, far_hi.astype(F32), NEG)
        hi = jnp.where(lane == i - 1, jnp.where(sel > 0.0, 0.0, NEG), hi)
        hi = jnp.where(lane == i, 0.0, hi)
        sel_up = pltpu.roll(sel, HALF_LANE, axis=1)
        lo = jnp.where((sel_up > 0.0) & (lane - HALF_LANE < i - 1), far_lo.astype(F32), 0.0)
        qbias = jnp.where(lane < HALF_LANE, hi, lo)
        qx_ref[h] = jnp.concatenate([qh, qbias.astype(BF16)], axis=-1)

    def step(j, near):
        rows = pl.ds(pl.multiple_of(j * tq, tq), tq)
        k = kx_ref[rows, :]
        v_ones = jnp.concatenate([v_ref[rows, :], ones], axis=-1)
        for h in range(N_HEADS):
            s = _dot_nt(qx_ref[h], k)
            if near == "own":
                row, col = _causal_ids(tq)
                s = jnp.where(col <= row, s + bias_ref[h * tq:(h + 1) * tq, tq:], NEG)
            elif near == "previous":
                s = s + bias_ref[h * tq:(h + 1) * tq, :tq]
            _flash_update(h, s, v_ones, m_ref, acc_ref)

    step(i, "own")

    @pl.when(i >= 1)
    def _():
        step(i - 1, "previous")

    n_far = jnp.maximum(i - 1, 0)

    def body(t, carry):
        step(2 * t, None)
        step(2 * t + 1, None)
        return carry

    lax.fori_loop(0, n_far // 2, body, 0)

    @pl.when(n_far % 2 == 1)
    def _():
        step(n_far - 1, None)

    for h in range(N_HEADS):
        o_ref[:, h * LANE:(h + 1) * LANE] = _group_lanes(_flash_result(h, acc_ref), h)


def _moba_prompt(t5_bias, mqh, mql, mkx, mvb, kmean, bias, t_p):
    tq = MB_BLOCK
    assert t_p // MB_BLOCK <= HALF_LANE
    qspec = pl.BlockSpec((tq, N_HEADS * LANE), lambda i: (i, 0))
    return pl.pallas_call(
        functools.partial(_moba_prompt_kernel, tq=tq),
        grid=(t_p // tq,),
        in_specs=[pl.BlockSpec(memory_space=pltpu.SMEM), qspec, qspec,
                  pl.BlockSpec((t_p, 2 * LANE), lambda i: (0, 0)),
                  pl.BlockSpec((t_p, LANE), lambda i: (0, 0)),
                  _full(kmean.shape), _full(bias.shape)],
        out_specs=pl.BlockSpec((tq, N_HEADS * LANE), lambda i: (i, 0)),
        out_shape=jax.ShapeDtypeStruct((t_p, N_HEADS * LANE), F32),
        scratch_shapes=[pltpu.VMEM((N_HEADS, tq, 2 * LANE), BF16), pltpu.VMEM((N_HEADS, tq, LANE), F32),
                        pltpu.VMEM((N_HEADS, tq, 2 * LANE), F32)],
        compiler_params=_cparams("parallel"),
        name="moba_prompt",
    )(t5_bias, mqh, mql, mkx, mvb, kmean, bias)


def _kmean_kernel(k_ref, o_ref):
    k = k_ref[...]
    nb = k.shape[0] // MB_BLOCK
    o_ref[...] = jnp.sum(k.reshape(nb, MB_BLOCK, LANE), axis=1) * (1.0 / MB_BLOCK)


def _block_means(mk, t_p):
    nb = t_p // MB_BLOCK
    per = _pick_tile(nb, (8,))
    return pl.pallas_call(
        _kmean_kernel,
        grid=(nb // per,),
        in_specs=[pl.BlockSpec((per * MB_BLOCK, LANE), lambda i: (i, 0))],
        out_specs=pl.BlockSpec((per, LANE), lambda i: (i, 0)),
        out_shape=jax.ShapeDtypeStruct((nb, LANE), F32),
        compiler_params=_cparams("parallel"),
        name="moba_kmean",
    )(mk)


def _t5_kernel(t5_ref, bucket_ref, o_ref):
    bucket = bucket_ref[...]
    for h in range(N_HEADS):
        acc = jnp.zeros(bucket.shape, F32)
        for b in range(T5_BUCKETS):
            acc = jnp.where(bucket == b, t5_ref[b, h], acc)
        o_ref[h] = acc


def _t5_tiles(t5_bias, bucket):
    return pl.pallas_call(
        _t5_kernel,
        in_specs=[pl.BlockSpec(memory_space=pltpu.SMEM), pl.BlockSpec(memory_space=pltpu.VMEM)],
        out_specs=pl.BlockSpec(memory_space=pltpu.VMEM),
        out_shape=jax.ShapeDtypeStruct((N_HEADS,) + bucket.shape, F32),
        name="t5_tiles",
    )(t5_bias, bucket)


PAGES_PER_STEP = 16
SAMPLE_ROWS = 4 * N_HEADS


def _page_specs(cache, layer, n_pages):
    def make(r):
        def index(b, c, pt):
            return (layer, pt[b * n_pages + c * PAGES_PER_STEP + r], 0, 0)
        return pl.BlockSpec((None, None) + cache.shape[2:], index)
    return [make(r) for r in range(PAGES_PER_STEP)]


def _feature_major(cache):
    l, pool, page = cache.shape[:3]
    perm = (0, 1) + tuple(range(3, cache.ndim)) + (2,)
    return cache.transpose(perm).reshape(l, pool, -1, page)


def _sample_tokens():
    return lax.broadcasted_iota(jnp.int32, (SAMPLE_ROWS, PAGE_SIZE), 0) // N_HEADS


def _own_group_lanes(x):
    rows = x.shape[0]
    lane = lax.broadcasted_iota(jnp.int32, (rows, LANE), 1)
    g = (lax.broadcasted_iota(jnp.int32, (rows, LANE), 0) % N_HEADS) // (N_HEADS // KV_HEADS)
    return jnp.where((lane >= HEAD_DIM * g) & (lane < HEAD_DIM * (g + 1)), x, 0.0)


def _last_step():
    return pl.program_id(1) == pl.num_programs(1) - 1


def _mla_sample_kernel(pt_ref, q_ref, knew_ref, *refs, n_pages):
    del pt_ref
    ckv_refs = refs[:PAGES_PER_STEP]
    kpe_refs = refs[PAGES_PER_STEP:2 * PAGES_PER_STEP]
    o_ref, s_ref, kbuf_ref = refs[2 * PAGES_PER_STEP:]
    c = pl.program_id(1)
    q = q_ref[...]
    q_lat = q[:, :MLA_KV_LORA]
    q_pe = q[:, MLA_KV_LORA:MLA_KV_LORA + MLA_ROPE]
    for r in range(PAGES_PER_STEP):
        pg = c * PAGES_PER_STEP + r
        ckv = ckv_refs[r][...].astype(BF16)
        kpe_t = kpe_refs[r][...].astype(BF16)
        kbuf_ref[pg] = ckv
        s_ref[pg] = (_dot_nt(q_lat, ckv) + _dot(q_pe, kpe_t)) * MLA_SCALE

    @pl.when(_last_step())
    def _():
        kn = knew_ref[...]
        s_new = _dot_nt(q, kn) * MLA_SCALE
        col = lax.broadcasted_iota(jnp.int32, s_new.shape, 1)
        s_new = jnp.where(col <= _sample_tokens(), s_new, NEG)
        s_all = s_ref[...]
        m = jnp.maximum(jnp.max(jnp.max(s_all, axis=0), axis=-1, keepdims=True),
                        jnp.max(s_new, axis=-1, keepdims=True))
        p_new = jnp.exp(s_new - m)
        p_all = jnp.exp(s_all - m)
        l = (jnp.sum(p_new, axis=-1, keepdims=True)
             + jnp.sum(jnp.sum(p_all, axis=0), axis=-1, keepdims=True))
        acc = _dot(p_new.astype(BF16), kn[:, :MLA_KV_LORA])
        for pg in range(n_pages):
            acc = acc + _dot(p_all[pg].astype(BF16), kbuf_ref[pg])
        o_ref[...] = acc / l


def _sample_grid(b, n_pages, in_specs, out_specs, scratch_shapes):
    return pltpu.PrefetchScalarGridSpec(
        num_scalar_prefetch=1, grid=(b, n_pages // PAGES_PER_STEP),
        in_specs=in_specs, out_specs=out_specs, scratch_shapes=scratch_shapes)


def _row_spec(width):
    return pl.BlockSpec((SAMPLE_ROWS, width), lambda i, c, pt: (i, 0))


def _new_spec(width):
    return pl.BlockSpec((None, PAGE_SIZE, width), lambda i, c, pt: (i, 0, 0))


def _mla_sample(pt, q, knew, cache_ckv, cache_kpe, layer):
    b = q.shape[0] // SAMPLE_ROWS
    n_pages = pt.shape[0] // b
    grid_spec = _sample_grid(
        b, n_pages,
        [_row_spec(2 * LANE), _new_spec(2 * LANE)]
        + _page_specs(cache_ckv, layer, n_pages) + _page_specs(cache_kpe, layer, n_pages),
        _row_spec(MLA_KV_LORA),
        [pltpu.VMEM((n_pages, SAMPLE_ROWS, PAGE_SIZE), F32),
         pltpu.VMEM((n_pages, PAGE_SIZE, MLA_KV_LORA), BF16)])
    return pl.pallas_call(
        functools.partial(_mla_sample_kernel, n_pages=n_pages),
        grid_spec=grid_spec,
        out_shape=jax.ShapeDtypeStruct((b * SAMPLE_ROWS, MLA_KV_LORA), F32),
        compiler_params=_cparams("parallel", "arbitrary"),
        name="mla_sample",
    )(pt, q, knew, *([cache_ckv] * PAGES_PER_STEP), *([cache_kpe] * PAGES_PER_STEP))


def _sb_sample_kernel(pt_ref, q_ref, knew_ref, vnew_ref, uu_ref, *refs, n_pages):
    del pt_ref
    k_refs = refs[:PAGES_PER_STEP]
    v_refs = refs[PAGES_PER_STEP:2 * PAGES_PER_STEP]
    o_ref, z_ref, vbuf_ref = refs[2 * PAGES_PER_STEP:]
    c = pl.program_id(1)
    q = q_ref[...]
    for r in range(PAGES_PER_STEP):
        pg = c * PAGES_PER_STEP + r
        z_ref[pg] = _dot(q, k_refs[r][...].astype(BF16))
        vbuf_ref[pg] = v_refs[r][...].astype(BF16)

    @pl.when(_last_step())
    def _():
        uu = uu_ref[...]
        ls, l1m = _log_sigmoid_pair(_dot_nt(q, knew_ref[...]))
        col = lax.broadcasted_iota(jnp.int32, ls.shape, 1)
        mask = col < _sample_tokens()
        l1m = jnp.where(mask, l1m, 0.0)
        within = _suffix_sum(l1m, uu)
        a = jnp.where(mask, jnp.exp(ls + within), 0.0)
        acc = _dot(a.astype(BF16), vnew_ref[...])
        carry = within[:, 0:1] + l1m[:, 0:1]
        ls, l1m = _log_sigmoid_pair(z_ref[...])
        within = _suffix_sum(l1m.reshape(n_pages * SAMPLE_ROWS, PAGE_SIZE), uu)
        within = within.reshape(n_pages, SAMPLE_ROWS, PAGE_SIZE)
        for pg in reversed(range(n_pages)):
            a = jnp.exp(ls[pg] + within[pg] + carry)
            acc = acc + _dot_nt(a.astype(BF16), vbuf_ref[pg])
            carry = carry + within[pg][:, 0:1] + l1m[pg][:, 0:1]
        o_ref[...] = _own_group_lanes(acc)


def _sb_sample(pt, q, knew, vnew, cache_k, cache_v, layer):
    b = q.shape[0] // SAMPLE_ROWS
    n_pages = pt.shape[0] // b
    grid_spec = _sample_grid(
        b, n_pages,
        [_row_spec(LANE), _new_spec(LANE), _new_spec(LANE),
         pl.BlockSpec((2 * PAGE_SIZE, PAGE_SIZE), lambda i, c, pt: (0, 0))]
        + _page_specs(cache_k, layer, n_pages) + _page_specs(cache_v, layer, n_pages),
        _row_spec(LANE),
        [pltpu.VMEM((n_pages, SAMPLE_ROWS, PAGE_SIZE), F32),
         pltpu.VMEM((n_pages, LANE, PAGE_SIZE), BF16)])
    return pl.pallas_call(
        functools.partial(_sb_sample_kernel, n_pages=n_pages),
        grid_spec=grid_spec,
        out_shape=jax.ShapeDtypeStruct((b * SAMPLE_ROWS, LANE), F32),
        compiler_params=_cparams("parallel", "arbitrary"),
        name="sb_sample",
    )(pt, q, knew, vnew, _suffix_matrix(PAGE_SIZE),
      *([cache_k] * PAGES_PER_STEP), *([cache_v] * PAGES_PER_STEP))


def _moba_sample_kernel(pt_ref, t5_ref, qh_ref, ql_ref, knew_ref, vnew_ref, bias_ref, *refs, n_pages):
    del pt_ref
    k_refs = refs[:PAGES_PER_STEP]
    v_refs = refs[PAGES_PER_STEP:2 * PAGES_PER_STEP]
    o_ref, s_ref, vbuf_ref, kmean_ref = refs[2 * PAGES_PER_STEP:]
    c = pl.program_id(1)
    qh = qh_ref[...]
    pages_per_block = MB_BLOCK // PAGE_SIZE
    n_blocks = n_pages // pages_per_block

    @pl.when(c == 0)
    def _():
        kmean_ref[...] = jnp.zeros(kmean_ref.shape, F32)

    blk_lane = lax.broadcasted_iota(jnp.int32, kmean_ref.shape, 1)
    for r in range(0, PAGES_PER_STEP, pages_per_block):
        ksum = jnp.zeros((LANE, 1), F32)
        for rr in range(r, r + pages_per_block):
            pg = c * PAGES_PER_STEP + rr
            k = k_refs[rr][...]
            s_ref[pg] = _dot(qh, k.astype(BF16))
            vbuf_ref[pg] = v_refs[rr][...].astype(BF16)
            ksum = ksum + jnp.sum(k, axis=1, keepdims=True)
        blk = c * (PAGES_PER_STEP // pages_per_block) + r // pages_per_block
        kmean_ref[...] += jnp.where(blk_lane == blk, ksum * (1.0 / MB_BLOCK), 0.0)

    @pl.when(_last_step())
    def _():
        kh, kl = _split_bf16(kmean_ref[...])
        score = _dot(qh, kh) + _dot(qh, kl) + _dot(ql_ref[...], kh)
        blk = lax.broadcasted_iota(jnp.int32, score.shape, 1)
        sel = _top3_select(score, blk < n_blocks)
        head = lax.broadcasted_iota(jnp.int32, (SAMPLE_ROWS, 1), 0) % N_HEADS
        far_bias = jnp.zeros((SAMPLE_ROWS, 1), F32)
        for h in range(N_HEADS):
            far_bias = jnp.where(head == h, t5_ref[T5_BUCKETS - 1, h], far_bias)
        bias = bias_ref[...]
        s_new = _dot_nt(qh, knew_ref[...]) + bias[:, MB_BLOCK:MB_BLOCK + PAGE_SIZE]
        col = lax.broadcasted_iota(jnp.int32, s_new.shape, 1)
        s_new = jnp.where(col <= _sample_tokens(), s_new, NEG)
        m = jnp.max(s_new, axis=-1, keepdims=True)
        masked = []
        for n in range(n_blocks):
            keep = sel[:, n:n + 1] > 0.0
            for pg in range(n * pages_per_block, (n + 1) * pages_per_block):
                if n == n_blocks - 1:
                    off = (pg - n * pages_per_block) * PAGE_SIZE
                    s = s_ref[pg] + bias[:, off:off + PAGE_SIZE]
                else:
                    s = s_ref[pg] + far_bias
                s = jnp.where(keep, s, NEG)
                masked.append(s)
                m = jnp.maximum(m, jnp.max(s, axis=-1, keepdims=True))
        p_new = jnp.exp(s_new - m)
        l = jnp.sum(p_new, axis=-1, keepdims=True)
        acc = _dot(p_new.astype(BF16), vnew_ref[...])
        for pg in range(n_pages):
            p = jnp.exp(masked[pg] - m)
            l = l + jnp.sum(p, axis=-1, keepdims=True)
            acc = acc + _dot_nt(p.astype(BF16), vbuf_ref[pg])
        o_ref[...] = _own_group_lanes(acc / l)


def _moba_sample(pt, t5_bias, qh, ql, knew, vnew, bias, cache_k, cache_v, layer, nbp):
    b = qh.shape[0] // SAMPLE_ROWS
    n_pages = pt.shape[0] // b
    grid_spec = _sample_grid(
        b, n_pages,
        [pl.BlockSpec(memory_space=pltpu.SMEM), _row_spec(LANE), _row_spec(LANE),
         _new_spec(LANE), _new_spec(LANE), pl.BlockSpec(bias.shape, lambda i, c, pt: (0, 0))]
        + _page_specs(cache_k, layer, n_pages) + _page_specs(cache_v, layer, n_pages),
        _row_spec(LANE),
        [pltpu.VMEM((n_pages, SAMPLE_ROWS, PAGE_SIZE), F32),
         pltpu.VMEM((n_pages, LANE, PAGE_SIZE), BF16),
         pltpu.VMEM((LANE, nbp), F32)])
    return pl.pallas_call(
        functools.partial(_moba_sample_kernel, n_pages=n_pages),
        grid_spec=grid_spec,
        out_shape=jax.ShapeDtypeStruct((b * SAMPLE_ROWS, LANE), F32),
        compiler_params=_cparams("parallel", "arbitrary"),
        name="moba_sample",
    )(pt, t5_bias, qh, ql, knew, vnew, bias,
      *([cache_k] * PAGES_PER_STEP), *([cache_v] * PAGES_PER_STEP))


CONV_HALO = 32


def _conv_prompt_kernel(cur_ref, prev_ref, w_ref, b_ref, g_ref, beta_ref, o_ref, ext_ref, *, tm):
    i = pl.program_id(0)
    prev = prev_ref[...]
    ext_ref[:CONV_HALO, :] = jnp.where(i > 0, prev, 0.0)
    ext_ref[CONV_HALO:, :] = cur_ref[...]
    acc = jnp.zeros((tm, CONV_CH), F32)
    base = CONV_HALO - (CONV_WIDTH - 1)
    for w in range(CONV_WIDTH):
        acc = acc + ext_ref[base + w:base + w + tm, :] * w_ref[w:w + 1, :]
    o_ref[...] = _silu(_layer_norm(acc + b_ref[...], g_ref[...], beta_ref[...]))


def _conv_prompt(glu, t_p, conv_w, conv_b, ln_g, ln_b):
    tm = _pick_tile(t_p, (512, 256, 128, 64, 32))
    per = tm // CONV_HALO
    return pl.pallas_call(
        functools.partial(_conv_prompt_kernel, tm=tm),
        grid=(t_p // tm,),
        in_specs=[pl.BlockSpec((tm, CONV_CH), lambda i: (i, 0)),
                  pl.BlockSpec((CONV_HALO, CONV_CH), lambda i: (jnp.maximum(i * per - 1, 0), 0)),
                  _full(conv_w.shape), _full(conv_b.shape), _full(ln_g.shape), _full(ln_b.shape)],
        out_specs=pl.BlockSpec((tm, CONV_CH), lambda i: (i, 0)),
        out_shape=jax.ShapeDtypeStruct((t_p, CONV_CH), F32),
        scratch_shapes=[pltpu.VMEM((tm + CONV_HALO, CONV_CH), F32)],
        compiler_params=_cparams("parallel"),
        name="conv_prompt",
    )(glu, glu, conv_w, conv_b, ln_g, ln_b)


def _conv_sample_kernel(ext_ref, w_ref, b_ref, g_ref, beta_ref, o_ref):
    n_new = o_ref.shape[0]
    for t in range(n_new):
        acc = jnp.zeros(o_ref.shape[1:], F32)
        for w in range(CONV_WIDTH):
            acc = acc + ext_ref[t + w] * w_ref[w:w + 1, :]
        o_ref[t] = _silu(_layer_norm(acc + b_ref[...], g_ref[...], beta_ref[...]))


def _conv_sample(ext_t, conv_w, conv_b, ln_g, ln_b):
    n_new = ext_t.shape[0] - (CONV_WIDTH - 1)
    vm = pl.BlockSpec(memory_space=pltpu.VMEM)
    return pl.pallas_call(
        _conv_sample_kernel,
        in_specs=[vm] * 5,
        out_specs=vm,
        out_shape=jax.ShapeDtypeStruct((n_new,) + ext_t.shape[1:], F32),
        name="conv_sample",
    )(ext_t, conv_w, conv_b, ln_g, ln_b)


def _merge_kernel(x_ref, cact_ref, lat_ref, sb_ref, mb_ref, gate_ref, gb_ref, wc_ref, wuv_ref, wm_ref,
                  ws_ref, wo_ref, wout_ref, g_ref, b_ref, o_ref, *, alpha):
    c = _dot(cact_ref[...].astype(BF16), wc_ref[...])
    mv = _dot(lat_ref[...].astype(BF16), wuv_ref[...])
    m = _dot(mv.astype(BF16), wm_ref[...])
    s = _dot(sb_ref[...].astype(BF16), ws_ref[...])
    o = _dot(mb_ref[...].astype(BF16), wo_ref[...])
    g = jax.nn.sigmoid(gate_ref[...] + gb_ref[...])
    d = D_MODEL
    y = g[:, :d] * c + g[:, d:2 * d] * m + g[:, 2 * d:3 * d] * s + g[:, 3 * d:] * o
    y = _dot(y.astype(BF16), wout_ref[...])
    o_ref[...] = _layer_norm(alpha * x_ref[...] + y, g_ref[...], b_ref[...])


def _merge(x, cact, lat, sb, mb, gate, gate_b, wc, wuv, wm, ws, wo, wout, ln_g, ln_b, alpha):
    n = x.shape[0]
    tm = _pick_tile(n, (256, 128, 64, 32, 16, 8))
    row = lambda w: pl.BlockSpec((tm, w), lambda i: (i, 0))
    weights = (gate_b, wc, wuv, wm, ws, wo, wout, ln_g, ln_b)
    return pl.pallas_call(
        functools.partial(_merge_kernel, alpha=alpha),
        grid=(n // tm,),
        in_specs=[row(a.shape[1]) for a in (x, cact, lat, sb, mb, gate)] + [_full(w.shape) for w in weights],
        out_specs=row(D_MODEL),
        out_shape=jax.ShapeDtypeStruct((n, D_MODEL), F32),
        compiler_params=_cparams("parallel"),
        name="merge",
    )(x, cact, lat, sb, mb, gate, *weights)


EXPERTS_PER_STEP = 4


def _first_index_of_max(s, idx):
    mx = jnp.max(s, axis=0, keepdims=True)
    first = jnp.min(jnp.where(s == mx, idx, 1e9), axis=0, keepdims=True)
    return idx == first


def _pick_column(x, j):
    lane = lax.broadcasted_iota(jnp.int32, x.shape, 1)
    return jnp.sum(jnp.where(lane == j, x, 0.0), axis=-1, keepdims=True)


def _router_kernel(x_ref, wh_ref, wl_ref, bias_ref, gate_ref, xb_ref):
    x = x_ref[...]
    xh, xl = _split_bf16(x)
    xb_ref[...] = xh
    logits = (_dot_nt(wh_ref[...], xh) + _dot_nt(wl_ref[...], xh) + _dot_nt(wh_ref[...], xl))[:N_EXPERTS]
    scores = jax.nn.sigmoid(logits)
    biased = scores + bias_ref[...]
    e, tm = biased.shape
    per = e // N_GROUPS
    grp = biased.reshape(N_GROUPS, per, tm)
    sub = lax.broadcasted_iota(jnp.int32, grp.shape, 1).astype(F32)
    top1 = jnp.max(grp, axis=1, keepdims=True)
    first = jnp.min(jnp.where(grp == top1, sub, 1e9), axis=1, keepdims=True)
    top2 = jnp.max(jnp.where(sub == first, -jnp.inf, grp), axis=1, keepdims=True)
    gs = jnp.broadcast_to(top1 + top2, grp.shape).reshape(e, tm)
    eidx = lax.broadcasted_iota(jnp.int32, biased.shape, 0)
    gidx = (eidx // per).astype(F32)
    e_keep = jnp.zeros(biased.shape, F32)
    for _ in range(TOPK_GROUPS):
        pick = _first_index_of_max(gs, gidx)
        e_keep = jnp.where(pick, 1.0, e_keep)
        gs = jnp.where(pick, -jnp.inf, gs)
    eidx = eidx.astype(F32)
    cand = jnp.where(e_keep > 0.0, biased, -jnp.inf)
    chosen = jnp.zeros(biased.shape, F32)
    for _ in range(TOP_K):
        pick = _first_index_of_max(cand, eidx)
        chosen = jnp.where(pick, 1.0, chosen)
        cand = jnp.where(pick, -jnp.inf, cand)
    sel = chosen * scores
    wts = sel / jnp.sum(sel, axis=0, keepdims=True) * ROUTED_SCALE
    gate_ref[...] = jnp.concatenate([wts, jnp.zeros_like(wts)], axis=0).T


def _router(x, router_w, router_bias):
    n = x.shape[0]
    tm = _pick_tile(n, (512, 256, 128))
    wt = jnp.pad(router_w.T, ((0, LANE - N_EXPERTS), (0, 0)))
    wh, wl = _split_bf16(wt)
    return pl.pallas_call(
        _router_kernel,
        grid=(n // tm,),
        in_specs=[pl.BlockSpec((tm, D_MODEL), lambda i: (i, 0)), _full(wh.shape), _full(wl.shape),
                  _full((N_EXPERTS, 1))],
        out_specs=[pl.BlockSpec((tm, LANE), lambda i: (i, 0)),
                   pl.BlockSpec((tm, D_MODEL), lambda i: (i, 0))],
        out_shape=[jax.ShapeDtypeStruct((n, LANE), F32), jax.ShapeDtypeStruct((n, D_MODEL), BF16)],
        compiler_params=_cparams("parallel"),
        name="moe_router",
    )(x, wh, wl, router_bias.reshape(N_EXPERTS, 1))


def _experts_kernel(x_ref, xb_ref, gate_ref, wg_ref, wu_ref, wd_ref, wsg_ref, wsu_ref, wsd_ref,
                    g_ref, b_ref, o_ref, acc_ref, *, alpha):
    step = pl.program_id(1)
    xb = xb_ref[...]

    @pl.when(step == 0)
    def _():
        h = _silu(_dot(xb, wsg_ref[...])) * _dot(xb, wsu_ref[...])
        acc_ref[...] = _dot(h.astype(BF16), wsd_ref[...])

    gate = gate_ref[...]
    hs = []
    for k in range(EXPERTS_PER_STEP):
        w = _pick_column(gate, step * EXPERTS_PER_STEP + k)
        h = _silu(_dot(xb, wg_ref[k])) * _dot(xb, wu_ref[k]) * w
        hs.append(h.astype(BF16))
    acc_ref[...] += _dot(jnp.concatenate(hs, axis=-1), wd_ref[...])

    @pl.when(step == pl.num_programs(1) - 1)
    def _():
        o_ref[...] = _layer_norm(alpha * x_ref[...] + acc_ref[...], g_ref[...], b_ref[...])


def _experts(x, xb, gate, wg, wu, wd, wsg, wsu, wsd, ln_g, ln_b, alpha):
    n = x.shape[0]
    tm = _pick_tile(n, (768, 512, 256, 128))
    n_exp, ff = wg.shape[0], wg.shape[2]
    wd = wd.reshape(n_exp // EXPERTS_PER_STEP, EXPERTS_PER_STEP * ff, D_MODEL)
    row = lambda w: pl.BlockSpec((tm, w), lambda i, e: (i, 0))
    return pl.pallas_call(
        functools.partial(_experts_kernel, alpha=alpha),
        grid=(n // tm, n_exp // EXPERTS_PER_STEP),
        in_specs=[row(D_MODEL), row(D_MODEL), row(LANE),
                  pl.BlockSpec((EXPERTS_PER_STEP, D_MODEL, ff), lambda i, e: (e, 0, 0)),
                  pl.BlockSpec((EXPERTS_PER_STEP, D_MODEL, ff), lambda i, e: (e, 0, 0)),
                  pl.BlockSpec((None, EXPERTS_PER_STEP * ff, D_MODEL), lambda i, e: (e, 0, 0)),
                  _full(wsg.shape), _full(wsu.shape), _full(wsd.shape), _full(ln_g.shape), _full(ln_b.shape)],
        out_specs=row(D_MODEL),
        out_shape=jax.ShapeDtypeStruct((n, D_MODEL), F32),
        scratch_shapes=[pltpu.VMEM((tm, D_MODEL), F32)],
        compiler_params=_cparams("parallel", "arbitrary"),
        name="moe_experts",
    )(x, xb, gate, wg, wu, wd, wsg, wsu, wsd, ln_g, ln_b)


def _vec(v):
    return v.reshape(1, -1)


def _pad_new(a):
    return jnp.pad(a, ((0, 0), (0, PAGE_SIZE - a.shape[1]), (0, 0))).astype(BF16)


def kernel(x_prompt, x_sample, cache_mla_ckv, cache_mla_kpe, cache_sb_k, cache_sb_v, cache_moba_k, cache_moba_v, state_conv, page_table, t5_bias, w_in, gate_b, conv_w, conv_b, conv_ln_g, conv_ln_b, w_conv_out, mla_q_norm, w_q_up, mla_kv_norm, w_kv_up, w_mla_out, w_sb_out, w_mb_out, w_out, ln1_g, ln1_b, router_w, router_bias, w_e_gate, w_e_up, w_e_down, w_s_gate, w_s_up, w_s_down, ln2_g, ln2_b):
    depth = w_in.shape[0]
    alpha = (2 * depth) ** 0.25
    bp, t_p, _ = x_prompt.shape
    assert bp == 1
    bs, n_new, _ = x_sample.shape
    n_s = bs * n_new
    n_pages = page_table.shape[1]
    past = n_pages * PAGE_SIZE
    assert n_new * N_HEADS == SAMPLE_ROWS
    assert t_p % MB_BLOCK == 0 and past % MB_BLOCK == 0 and n_pages % PAGES_PER_STEP == 0

    x = jnp.concatenate([x_prompt.reshape(t_p, D_MODEL), x_sample.reshape(n_s, D_MODEL)], axis=0)
    pos = jnp.concatenate([jnp.arange(t_p, dtype=jnp.int32),
                           jnp.tile(past + jnp.arange(n_new, dtype=jnp.int32), bs)])
    cos, sin = _rope_tables(pos)
    pt = page_table.reshape(-1)
    kpe_t, sbk_t, sbv_t, mbk_t, mbv_t = (
        _feature_major(c) for c in (cache_mla_kpe, cache_sb_k, cache_sb_v, cache_moba_k, cache_moba_v))

    tq = MB_BLOCK
    rel_p = jnp.arange(tq)[:, None] - jnp.arange(2 * tq)[None, :] + tq
    bias_p = _t5_tiles(t5_bias, _t5_bucket(rel_p).astype(jnp.int32)).reshape(N_HEADS * tq, 2 * tq)
    kpos_s = jnp.concatenate([past - MB_BLOCK + jnp.arange(MB_BLOCK), past + jnp.arange(PAGE_SIZE)])
    rel_s = past + jnp.arange(8)[:, None] - kpos_s[None, :]
    bias_s = _t5_tiles(t5_bias, _t5_bucket(rel_s).astype(jnp.int32))
    bias_s = bias_s[:, :n_new].transpose(1, 0, 2).reshape(SAMPLE_ROWS, MB_BLOCK + PAGE_SIZE)

    nb_p = t_p // MB_BLOCK
    nb_s = past // MB_BLOCK
    nbp_p = -(-nb_p // LANE) * LANE
    nbp_s = -(-nb_s // LANE) * LANE

    new_p = {k: [] for k in ("ckv", "kpe", "sk", "sv", "mk", "mv", "conv")}
    new_s = {k: [] for k in new_p}
    for l in range(depth):
        w1, wg, wq, wuk, wuv = _pack_layer_weights(w_in[l], w_q_up[l], w_kv_up[l])
        pc = _project(x, cos, sin, w1, wg, wq, wuk, _vec(mla_q_norm[l]), _vec(mla_kv_norm[l]), t_p)
        for k in ("ckv", "kpe", "sk", "sv", "mk", "mv"):
            new_p[k].append(pc[k][:t_p])
            new_s[k].append(pc[k][t_p:])

        lat_p = _mla_prompt(pc["qcat"], pc["kcat"], t_p, _pick_tile(t_p, (256, 128)))
        sb_p = _sb_prompt(pc["sq"], pc["skv"], t_p, _pick_tile(t_p, (256, 128)))
        kmean = jnp.pad(_block_means(pc["mk"], t_p), ((0, nbp_p - nb_p), (0, 0)))
        mb_p = _moba_prompt(t5_bias, pc["mqh"], pc["mql"], pc["mkx"], pc["mvb"], kmean, bias_p, t_p)
        glu_p = pc["glu"][:t_p]
        cact_p = _conv_prompt(pc["glu"], t_p, conv_w[l], _vec(conv_b[l]), _vec(conv_ln_g[l]), _vec(conv_ln_b[l]))
        new_p["conv"].append(glu_p[t_p - (CONV_WIDTH - 1):])

        def rows(a, width):
            return a[t_p:].reshape(n_s * N_HEADS, width)

        def newkeys(a):
            return _pad_new(a[t_p:].reshape(bs, n_new, a.shape[1]))

        lat_s = _mla_sample(pt, rows(pc["qcat"], 2 * LANE), newkeys(pc["kcat"]),
                            cache_mla_ckv, kpe_t, l)
        skv_new = newkeys(pc["skv"])
        sb_s = _sb_sample(pt, rows(pc["sq"], LANE), skv_new[..., :LANE], skv_new[..., LANE:], sbk_t, sbv_t, l)
        mb_s = _moba_sample(pt, t5_bias, rows(pc["mqh"], LANE), rows(pc["mql"], LANE),
                            newkeys(pc["mkx"])[..., :LANE], newkeys(pc["mvb"]), bias_s, mbk_t, mbv_t, l, nbp_s)
        ext_t = jnp.concatenate([state_conv[l].transpose(1, 0, 2),
                                 pc["glu"][t_p:].reshape(bs, n_new, CONV_CH).transpose(1, 0, 2)], axis=0)
        new_s["conv"].append(ext_t[n_new:].transpose(1, 0, 2))
        cact_s = _conv_sample(ext_t, conv_w[l], _vec(conv_b[l]), _vec(conv_ln_g[l]), _vec(conv_ln_b[l]))
        cact_s = cact_s.transpose(1, 0, 2).reshape(n_s, CONV_CH)

        cact = jnp.concatenate([cact_p, cact_s], axis=0)
        lat = jnp.concatenate([lat_p, lat_s.reshape(n_s, MLA_HEADS * MLA_KV_LORA)], axis=0)
        sb = jnp.concatenate([sb_p, sb_s.reshape(n_s, N_HEADS * LANE)], axis=0)
        mb = jnp.concatenate([mb_p, mb_s.reshape(n_s, N_HEADS * LANE)], axis=0)
        x1 = _merge(x, cact, lat, sb, mb, pc["gate"], _vec(gate_b[l]),
                    w_conv_out[l].astype(BF16), wuv, w_mla_out[l].astype(BF16),
                    _head_pad_cols(w_sb_out[l].T).T.astype(BF16), _head_pad_cols(w_mb_out[l].T).T.astype(BF16),
                    w_out[l].astype(BF16), _vec(ln1_g[l]), _vec(ln1_b[l]), alpha)
        gate, xb = _router(x1, router_w[l], router_bias[l])
        x = _experts(x1, xb, gate, w_e_gate[l].astype(BF16), w_e_up[l].astype(BF16), w_e_down[l].astype(BF16),
                     w_s_gate[l].astype(BF16), w_s_up[l].astype(BF16), w_s_down[l].astype(BF16),
                     _vec(ln2_g[l]), _vec(ln2_b[l]), alpha)

    def stack_p(k, tail):
        return jnp.stack(new_p[k]).reshape((depth, 1, t_p) + tail)

    def stack_s(k, tail):
        return jnp.stack(new_s[k]).reshape((depth, bs, n_new) + tail)

    kv = (KV_HEADS, HEAD_DIM)
    return (x[:t_p].reshape(1, t_p, D_MODEL), x[t_p:].reshape(bs, n_new, D_MODEL),
            stack_p("ckv", (MLA_KV_LORA,)), stack_s("ckv", (MLA_KV_LORA,)),
            stack_p("kpe", (MLA_ROPE,)), stack_s("kpe", (MLA_ROPE,)),
            stack_p("sk", kv), stack_s("sk", kv), stack_p("sv", kv), stack_s("sv", kv),
            stack_p("mk", kv), stack_s("mk", kv), stack_p("mv", kv), stack_s("mv", kv),
            jnp.stack(new_p["conv"]).reshape(depth, 1, CONV_WIDTH - 1, CONV_CH),
            jnp.stack(new_s["conv"]))
```

```python
import functools
import math

import numpy as np
import jax
import jax.numpy as jnp
from jax import lax
from jax.experimental import pallas as pl
from jax.experimental.pallas import tpu as pltpu

F32 = jnp.float32
BF16 = jnp.bfloat16

D_MODEL = 1024
HEAD_DIM = 64
CONV_CH = 256
CONV_WIDTH = 31
MLA_HEADS = 4
MLA_Q_LORA = 256
MLA_KV_LORA = 128
MLA_NOPE = 64
MLA_ROPE = 32
MLA_V = 64
MLA_SCALE = (MLA_NOPE + MLA_ROPE) ** -0.5
ROPE_THETA = 10000.0
N_HEADS = 4
KV_HEADS = 2
QK_SCALE = HEAD_DIM ** -0.5
MB_BLOCK = 256
MB_TOPK = 3
T5_BUCKETS = 32
T5_MAX_DIST = 128
N_BRANCH = 4
N_EXPERTS = 64
EXPERT_FF = 256
SHARED_FF = 256
TOP_K = 8
N_GROUPS = 8
TOPK_GROUPS = 4
ROUTED_SCALE = 2.5
LN_EPS = 1e-5
PAGE_SIZE = 128

LANE = 128
HALF_LANE = LANE // 2
NEG = -1e30
LOG2_E = math.log2(math.e)
VMEM_LIMIT = 56 * 1024 * 1024

NT_DIMS = (((1,), (1,)), ((), ()))


def _cparams(*sem):
    return pltpu.CompilerParams(dimension_semantics=sem, vmem_limit_bytes=VMEM_LIMIT)


def _pick_tile(n, cands):
    for c in cands:
        if n % c == 0:
            return c
    raise ValueError(f"no tile for {n}")


def _full(shape):
    nd = len(shape)
    return pl.BlockSpec(shape, lambda *_: (0,) * nd)


def _dot(a, b):
    return jnp.dot(a, b, preferred_element_type=F32)


def _dot_nt(a, b):
    return lax.dot_general(a, b, NT_DIMS, preferred_element_type=F32)


def _split_bf16(x):
    hi = x.astype(BF16)
    lo = (x - hi.astype(F32)).astype(BF16)
    return hi, lo


def _layer_norm(x, g, b):
    mu = jnp.mean(x, axis=-1, keepdims=True)
    d = x - mu
    var = jnp.mean(d * d, axis=-1, keepdims=True)
    return d * lax.rsqrt(var + LN_EPS) * g + b


def _rms_norm(x, g):
    return x * lax.rsqrt(jnp.mean(x * x, axis=-1, keepdims=True) + LN_EPS) * g


def _silu(x):
    return x * jax.nn.sigmoid(x)


def _head_pad_cols(w):
    k = w.shape[0]
    wh = w.reshape(k, N_HEADS, HEAD_DIM)
    z = jnp.zeros_like(wh)
    per_head = [jnp.concatenate([wh[:, h], z[:, h]] if h // 2 == 0 else [z[:, h], wh[:, h]], axis=-1)
                for h in range(N_HEADS)]
    return jnp.concatenate(per_head, axis=-1)


def _rot_cols(w):
    half = w.shape[-1] // 2
    return jnp.concatenate([-w[..., half:], w[..., :half]], axis=-1)


def _pad_cols(w, width):
    return jnp.pad(w, ((0, 0), (0, width - w.shape[-1])))


C_CONV, C_CQ, C_CKV, C_SQ, C_SK, C_SV, C_MQ, C_MK, C_MV, C_KPE, C_KROT, C_END = (
    0, 512, 768, 896, 1408, 1536, 1664, 2176, 2304, 2432, 2560, 2688)


def _pack_layer_weights(w_in, w_q_up, w_kv_up):
    pts = np.cumsum((2 * CONV_CH, MLA_Q_LORA, MLA_KV_LORA, MLA_ROPE, 256, 128, 128, 256, 128, 128))
    cv, cq, ckv, kpe, sq, sk, sv, mq, mk, mv, gate = jnp.split(w_in, pts.tolist(), axis=-1)
    w1 = jnp.concatenate([
        cv, cq, ckv, _head_pad_cols(sq), sk, sv, _head_pad_cols(mq), mk, mv,
        _pad_cols(kpe, LANE), _pad_cols(_rot_cols(kpe), LANE)], axis=-1).astype(BF16)
    wq = w_q_up.reshape(MLA_Q_LORA, MLA_HEADS, MLA_NOPE + MLA_ROPE)
    nope = wq[:, :, :MLA_NOPE].reshape(MLA_Q_LORA, MLA_HEADS * MLA_NOPE)
    pe = [_pad_cols(wq[:, h, MLA_NOPE:], LANE) for h in range(MLA_HEADS)]
    rot = [_pad_cols(_rot_cols(wq[:, h, MLA_NOPE:]), LANE) for h in range(MLA_HEADS)]
    wq_p = jnp.concatenate([nope] + pe + rot, axis=-1).astype(BF16)
    wkv = w_kv_up.reshape(MLA_KV_LORA, MLA_HEADS, MLA_NOPE + MLA_V)
    wuk = jnp.zeros((MLA_HEADS * MLA_NOPE, MLA_HEADS * MLA_KV_LORA), F32)
    wuv = jnp.zeros((MLA_HEADS * MLA_KV_LORA, MLA_HEADS * MLA_V), F32)
    for h in range(MLA_HEADS):
        wuk = wuk.at[h * MLA_NOPE:(h + 1) * MLA_NOPE, h * MLA_KV_LORA:(h + 1) * MLA_KV_LORA].set(
            wkv[:, h, :MLA_NOPE].T)
        wuv = wuv.at[h * MLA_KV_LORA:(h + 1) * MLA_KV_LORA, h * MLA_V:(h + 1) * MLA_V].set(
            wkv[:, h, MLA_NOPE:])
    return w1, gate.astype(BF16), wq_p, wuk.astype(BF16), wuv.astype(BF16)


def _rope_tables(pos):
    half = MLA_ROPE // 2
    inv = ROPE_THETA ** (-jnp.arange(half, dtype=F32) / half)
    ang = pos.astype(F32)[:, None] * inv[None, :]
    cos = jnp.tile(jnp.cos(ang), (1, 2 * MLA_HEADS))
    sin = jnp.tile(jnp.sin(ang), (1, 2 * MLA_HEADS))
    return cos, sin


def _t5_bucket(rel):
    n = jnp.maximum(rel, 0)
    exact = T5_BUCKETS // 2
    nf = jnp.maximum(n, 1).astype(F32)
    large = exact + (jnp.log(nf / exact) / math.log(T5_MAX_DIST / exact)
                     * (T5_BUCKETS - exact)).astype(jnp.int32)
    large = jnp.minimum(large, T5_BUCKETS - 1)
    return jnp.where(n < exact, n, large)


def _proj_kernel(x_ref, cos_ref, sin_ref, w1_ref, wg_ref, wq_ref, wuk_ref, qn_ref, kvn_ref,
                 glu_ref, ckv_ref, kpe_ref, kcat_ref, qcat_ref, sq_ref, sk_ref, sv_ref, skv_ref,
                 mqh_ref, mql_ref, mk_ref, mv_ref, mkx_ref, mvb_ref, gate_ref, *, tm, t_p):
    x = x_ref[...].astype(BF16)
    u = _dot(x, w1_ref[...])
    gate_ref[...] = _dot(x, wg_ref[...])
    cos = cos_ref[...]
    sin = sin_ref[...]
    glu_ref[...] = u[:, C_CONV:C_CONV + CONV_CH] * jax.nn.sigmoid(u[:, C_CONV + CONV_CH:C_CQ])
    ckvn = _rms_norm(u[:, C_CKV:C_SQ], kvn_ref[...])
    ckv_ref[...] = ckvn
    kpe = u[:, C_KPE:C_KROT] * cos + u[:, C_KROT:C_END] * sin
    kpe_ref[...] = kpe[:, :MLA_ROPE]
    kcat_ref[...] = jnp.concatenate([ckvn, kpe], axis=-1).astype(BF16)
    cqn = _rms_norm(u[:, C_CQ:C_CKV], qn_ref[...])
    qa = _dot(cqn.astype(BF16), wq_ref[...])
    nq = MLA_HEADS * MLA_NOPE
    qlat = _dot(qa[:, :nq].astype(BF16), wuk_ref[...])
    parts = []
    for h in range(MLA_HEADS):
        pe = (qa[:, nq + LANE * h:nq + LANE * (h + 1)] * cos
              + qa[:, nq + LANE * (MLA_HEADS + h):nq + LANE * (MLA_HEADS + h + 1)] * sin)
        parts += [qlat[:, MLA_KV_LORA * h:MLA_KV_LORA * (h + 1)], pe]
    qcat_ref[...] = jnp.concatenate(parts, axis=-1).astype(BF16)
    sq_ref[...] = (u[:, C_SQ:C_SK] * QK_SCALE).astype(BF16)
    sk = u[:, C_SK:C_SV]
    sv = u[:, C_SV:C_MQ]
    sk_ref[...] = sk
    sv_ref[...] = sv
    skv_ref[...] = jnp.concatenate([sk, sv], axis=-1).astype(BF16)
    mqh, mql = _split_bf16(u[:, C_MQ:C_MK] * QK_SCALE)
    mqh_ref[...] = mqh
    mql_ref[...] = mql
    mk = u[:, C_MK:C_MV]
    mv = u[:, C_MV:C_KPE]
    mk_ref[...] = mk
    mv_ref[...] = mv
    mvb_ref[...] = mv.astype(BF16)
    row = pl.program_id(0) * tm + lax.broadcasted_iota(jnp.int32, (tm, LANE), 0)
    lane = lax.broadcasted_iota(jnp.int32, (tm, LANE), 1)
    blk = row // MB_BLOCK
    onehot = jnp.where(lane == blk, 1.0, jnp.where(lane == blk + HALF_LANE, 1.0, 0.0))
    onehot = jnp.where(row < t_p, onehot, 0.0)
    mkx_ref[...] = jnp.concatenate([mk, onehot], axis=-1).astype(BF16)


def _project(x, cos, sin, w1, wg, wq, wuk, qn, kvn, t_p):
    n = x.shape[0]
    tm = _pick_tile(n, (256, 128, 64, 32, 16, 8))
    row = lambda w: pl.BlockSpec((tm, w), lambda i: (i, 0))
    outs = [("glu", CONV_CH, F32), ("ckv", MLA_KV_LORA, F32), ("kpe", MLA_ROPE, F32),
            ("kcat", 2 * LANE, BF16), ("qcat", MLA_HEADS * 2 * LANE, BF16),
            ("sq", N_HEADS * LANE, BF16), ("sk", LANE, F32), ("sv", LANE, F32), ("skv", 2 * LANE, BF16),
            ("mqh", N_HEADS * LANE, BF16), ("mql", N_HEADS * LANE, BF16),
            ("mk", LANE, F32), ("mv", LANE, F32), ("mkx", 2 * LANE, BF16), ("mvb", LANE, BF16),
            ("gate", N_BRANCH * D_MODEL, F32)]
    res = pl.pallas_call(
        functools.partial(_proj_kernel, tm=tm, t_p=t_p),
        grid=(n // tm,),
        in_specs=[row(D_MODEL), row(LANE), row(LANE), _full(w1.shape), _full(wg.shape), _full(wq.shape),
                  _full(wuk.shape), _full(qn.shape), _full(kvn.shape)],
        out_specs=[row(w) for _, w, _ in outs],
        out_shape=[jax.ShapeDtypeStruct((n, w), dt) for _, w, dt in outs],
        compiler_params=_cparams("parallel"),
        name="proj",
    )(x, cos, sin, w1, wg, wq, wuk, qn, kvn)
    return {name: r for (name, _, _), r in zip(outs, res)}


def _causal_ids(tq):
    row = lax.broadcasted_iota(jnp.int32, (tq, tq), 0)
    col = lax.broadcasted_iota(jnp.int32, (tq, tq), 1)
    return row, col


def _flash_init(m_ref, acc_ref):
    m_ref[...] = jnp.full(m_ref.shape, NEG, F32)
    acc_ref[...] = jnp.zeros(acc_ref.shape, F32)


def _flash_update(h, s, v_ones, m_ref, acc_ref, base2=False):
    ex = jnp.exp2 if base2 else jnp.exp
    m_prev = m_ref[h]
    m_new = jnp.maximum(m_prev, jnp.max(s, axis=-1, keepdims=True))
    p = ex(s - jnp.tile(m_new, (1, s.shape[1] // LANE)))
    alpha = ex(m_prev - m_new)
    acc_ref[h] = jnp.tile(alpha, (1, 2)) * acc_ref[h] + _dot(p.astype(BF16), v_ones)
    m_ref[h] = m_new


BLOCKS_PER_TRIP = 4


def _unrolled_blocks(n, step):
    u = BLOCKS_PER_TRIP

    def body(t, carry):
        for k in range(u):
            step(u * t + k)
        return carry

    lax.fori_loop(0, n // u, body, 0)
    for k in range(u - 1):
        @pl.when(n % u > k)
        def _():
            step((n // u) * u + k)


def _flash_result(h, acc_ref):
    acc = acc_ref[h]
    return acc[:, :LANE] / acc[:, LANE:]


def _mla_prompt_kernel(q_ref, k_ref, o_ref, m_ref, acc_ref, *, tq):
    i = pl.program_id(0)
    _flash_init(m_ref, acc_ref)
    ones = jnp.ones((tq, LANE), BF16)

    def step(j, masked):
        k = k_ref[pl.ds(pl.multiple_of(j * tq, tq), tq), :]
        v_ones = jnp.concatenate([k[:, :MLA_KV_LORA], ones], axis=-1)
        for h in range(MLA_HEADS):
            s = _dot_nt(q_ref[:, h * 2 * LANE:(h + 1) * 2 * LANE], k) * (MLA_SCALE * LOG2_E)
            if masked:
                row, col = _causal_ids(tq)
                s = jnp.where(col <= row, s, NEG)
            _flash_update(h, s, v_ones, m_ref, acc_ref, base2=True)

    _unrolled_blocks(i, lambda j: step(j, False))
    step(i, True)
    for h in range(MLA_HEADS):
        o_ref[:, h * MLA_KV_LORA:(h + 1) * MLA_KV_LORA] = _flash_result(h, acc_ref)


def _mla_prompt(qcat, kcat, t_p, tq):
    return pl.pallas_call(
        functools.partial(_mla_prompt_kernel, tq=tq),
        grid=(t_p // tq,),
        in_specs=[pl.BlockSpec((tq, MLA_HEADS * 2 * LANE), lambda i: (i, 0)),
                  pl.BlockSpec((t_p, 2 * LANE), lambda i: (0, 0))],
        out_specs=pl.BlockSpec((tq, MLA_HEADS * MLA_KV_LORA), lambda i: (i, 0)),
        out_shape=jax.ShapeDtypeStruct((t_p, MLA_HEADS * MLA_KV_LORA), F32),
        scratch_shapes=[pltpu.VMEM((MLA_HEADS, tq, LANE), F32), pltpu.VMEM((MLA_HEADS, tq, 2 * LANE), F32)],
        compiler_params=_cparams("parallel"),
        name="mla_prompt",
    )(qcat, kcat)


def _log_sigmoid_pair(z):
    ls = jnp.minimum(z, 0.0) - jnp.log(1.0 + jnp.exp(-jnp.abs(z)))
    return ls, ls - z


def _suffix_sum(l1m, uu):
    hi, lo = _split_bf16(l1m)
    return _dot(jnp.concatenate([hi, lo], axis=-1), uu)


def _group_lanes(x, h):
    lane = lax.broadcasted_iota(jnp.int32, x.shape, 1)
    g = h // (N_HEADS // KV_HEADS)
    return jnp.where((lane >= HEAD_DIM * g) & (lane < HEAD_DIM * (g + 1)), x, 0.0)


def _sb_prompt_kernel(q_ref, kv_ref, uu_ref, o_ref, qs_ref, c_ref, acc_ref, *, tq):
    i = pl.program_id(0)
    rows = N_HEADS * tq
    for h in range(N_HEADS):
        qs_ref[h * tq:(h + 1) * tq, :] = q_ref[:, h * LANE:(h + 1) * LANE]
    c_ref[...] = jnp.zeros(c_ref.shape, F32)
    acc_ref[...] = jnp.zeros(acc_ref.shape, F32)

    def step(j, masked):
        kv = kv_ref[pl.ds(pl.multiple_of(j * tq, tq), tq), :]
        z = _dot_nt(qs_ref[...], kv[:, :LANE])
        ls, l1m = _log_sigmoid_pair(z)
        if masked:
            row = lax.broadcasted_iota(jnp.int32, (rows, tq), 0) & (tq - 1)
            col = lax.broadcasted_iota(jnp.int32, (rows, tq), 1)
            mask = col < row
            l1m = jnp.where(mask, l1m, 0.0)
        within = _suffix_sum(l1m, uu_ref[...])
        a = jnp.exp(ls + within + c_ref[...])
        if masked:
            a = jnp.where(mask, a, 0.0)
        acc_ref[...] += _dot(a.astype(BF16), kv[:, LANE:])
        c_ref[...] += within[:, 0:1] + l1m[:, 0:1]

    step(i, True)

    def body(t, carry):
        step(i - 1 - 2 * t, False)
        step(i - 2 - 2 * t, False)
        return carry

    lax.fori_loop(0, i // 2, body, 0)

    @pl.when(i % 2 == 1)
    def _():
        step(0, False)

    acc = acc_ref[...]
    for h in range(N_HEADS):
        o_ref[:, h * LANE:(h + 1) * LANE] = _group_lanes(acc[h * tq:(h + 1) * tq], h)


def _suffix_matrix(tk):
    u = (np.arange(tk)[:, None] > np.arange(tk)[None, :]).astype(np.float32)
    return jnp.asarray(np.concatenate([u, u], axis=0), dtype=BF16)


def _sb_prompt(sq, skv, t_p, tq):
    return pl.pallas_call(
        functools.partial(_sb_prompt_kernel, tq=tq),
        grid=(t_p // tq,),
        in_specs=[pl.BlockSpec((tq, N_HEADS * LANE), lambda i: (i, 0)),
                  pl.BlockSpec((t_p, 2 * LANE), lambda i: (0, 0)),
                  _full((2 * tq, tq))],
        out_specs=pl.BlockSpec((tq, N_HEADS * LANE), lambda i: (i, 0)),
        out_shape=jax.ShapeDtypeStruct((t_p, N_HEADS * LANE), F32),
        scratch_shapes=[pltpu.VMEM((N_HEADS * tq, LANE), BF16), pltpu.VMEM((N_HEADS * tq, 1), F32),
                        pltpu.VMEM((N_HEADS * tq, LANE), F32)],
        compiler_params=_cparams("parallel"),
        name="sb_prompt",
    )(sq, skv, _suffix_matrix(tq))


def _top3_select(score, valid):
    lane = lax.broadcasted_iota(jnp.int32, score.shape, 1).astype(F32)
    s = jnp.where(valid, score, -jnp.inf)
    sel = jnp.zeros(score.shape, F32)
    for _ in range(MB_TOPK):
        mx = jnp.max(s, axis=-1, keepdims=True)
        cand = jnp.where(s == mx, lane, 1e9)
        cand = jnp.where(mx > -jnp.inf, cand, 1e9)
        pick = lane == jnp.min(cand, axis=-1, keepdims=True)
        sel = jnp.where(pick, 1.0, sel)
        s = jnp.where(pick, -jnp.inf, s)
    return sel


def _block_scores(qh, ql, kmean):
    kh, kl = _split_bf16(kmean)
    return _dot_nt(qh, kh) + _dot_nt(qh, kl) + _dot_nt(ql, kh)


def _moba_prompt_kernel(t5_ref, qh_ref, ql_ref, kx_ref, v_ref, kmean_ref, bias_ref, o_ref,
                        qx_ref, m_ref, acc_ref, *, tq):
    i = pl.program_id(0)
    _flash_init(m_ref, acc_ref)
    ones = jnp.ones((tq, LANE), BF16)
    lane = lax.broadcasted_iota(jnp.int32, (tq, LANE), 1)
    kmean = kmean_ref[...]
    for h in range(N_HEADS):
        qh = qh_ref[:, h * LANE:(h + 1) * LANE]
        sel = _top3_select(_block_scores(qh, ql_ref[:, h * LANE:(h + 1) * LANE], kmean), lane < i)
        far_hi, far_lo = _split_bf16(jnp.full((tq, LANE), t5_ref[T5_BUCKETS - 1, h], F32))
        hi = jnp.where(sel > 0.0, far_hi.astype(F32), NEG)
        hi = jnp.where(lane == i - 1, jnp.where(sel > 0.0, 0.0, NEG), hi)
        hi = jnp.where(lane == i, 0.0, hi)
        sel_up = pltpu.roll(sel, HALF_LANE, axis=1)
        lo = jnp.where((sel_up > 0.0) & (lane - HALF_LANE < i - 1), far_lo.astype(F32), 0.0)
        qbias = jnp.where(lane < HALF_LANE, hi, lo)
        qx_ref[h] = jnp.concatenate([qh, qbias.astype(BF16)], axis=-1)

    def step(j, near):
        rows = pl.ds(pl.multiple_of(j * tq, tq), tq)
        k = kx_ref[rows, :]
        v_ones = jnp.concatenate([v_ref[rows, :], ones], axis=-1)
        for h in range(N_HEADS):
            s = _dot_nt(qx_ref[h], k)
            if near == "own":
                row, col = _causal_ids(tq)
                s = jnp.where(col <= row, s + bias_ref[h * tq:(h + 1) * tq, tq:], NEG)
            elif near == "previous":
                s = s + bias_ref[h * tq:(h + 1) * tq, :tq]
            _flash_update(h, s, v_ones, m_ref, acc_ref)

    step(i, "own")

    @pl.when(i >= 1)
    def _():
        step(i - 1, "previous")

    _unrolled_blocks(jnp.maximum(i - 1, 0), lambda j: step(j, None))
    for h in range(N_HEADS):
        o_ref[:, h * LANE:(h + 1) * LANE] = _group_lanes(_flash_result(h, acc_ref), h)


def _moba_prompt(t5_bias, mqh, mql, mkx, mvb, kmean, bias, t_p):
    tq = MB_BLOCK
    assert t_p // MB_BLOCK <= HALF_LANE
    qspec = pl.BlockSpec((tq, N_HEADS * LANE), lambda i: (i, 0))
    return pl.pallas_call(
        functools.partial(_moba_prompt_kernel, tq=tq),
        grid=(t_p // tq,),
        in_specs=[pl.BlockSpec(memory_space=pltpu.SMEM), qspec, qspec,
                  pl.BlockSpec((t_p, 2 * LANE), lambda i: (0, 0)),
                  pl.BlockSpec((t_p, LANE), lambda i: (0, 0)),
                  _full(kmean.shape), _full(bias.shape)],
        out_specs=pl.BlockSpec((tq, N_HEADS * LANE), lambda i: (i, 0)),
        out_shape=jax.ShapeDtypeStruct((t_p, N_HEADS * LANE), F32),
        scratch_shapes=[pltpu.VMEM((N_HEADS, tq, 2 * LANE), BF16), pltpu.VMEM((N_HEADS, tq, LANE), F32),
                        pltpu.VMEM((N_HEADS, tq, 2 * LANE), F32)],
        compiler_params=_cparams("parallel"),
        name="moba_prompt",
    )(t5_bias, mqh, mql, mkx, mvb, kmean, bias)


def _kmean_kernel(k_ref, o_ref):
    k = k_ref[...]
    nb = k.shape[0] // MB_BLOCK
    o_ref[...] = jnp.sum(k.reshape(nb, MB_BLOCK, LANE), axis=1) * (1.0 / MB_BLOCK)


def _block_means(mk, t_p):
    nb = t_p // MB_BLOCK
    per = _pick_tile(nb, (8,))
    return pl.pallas_call(
        _kmean_kernel,
        grid=(nb // per,),
        in_specs=[pl.BlockSpec((per * MB_BLOCK, LANE), lambda i: (i, 0))],
        out_specs=pl.BlockSpec((per, LANE), lambda i: (i, 0)),
        out_shape=jax.ShapeDtypeStruct((nb, LANE), F32),
        compiler_params=_cparams("parallel"),
        name="moba_kmean",
    )(mk)


def _t5_kernel(t5_ref, bucket_ref, o_ref):
    bucket = bucket_ref[...]
    for h in range(N_HEADS):
        acc = jnp.zeros(bucket.shape, F32)
        for b in range(T5_BUCKETS):
            acc = jnp.where(bucket == b, t5_ref[b, h], acc)
        o_ref[h] = acc


def _t5_tiles(t5_bias, bucket):
    return pl.pallas_call(
        _t5_kernel,
        in_specs=[pl.BlockSpec(memory_space=pltpu.SMEM), pl.BlockSpec(memory_space=pltpu.VMEM)],
        out_specs=pl.BlockSpec(memory_space=pltpu.VMEM),
        out_shape=jax.ShapeDtypeStruct((N_HEADS,) + bucket.shape, F32),
        name="t5_tiles",
    )(t5_bias, bucket)


CHUNK_PAGES = 16
RING_SLOTS = 3
SAMPLE_ROWS = 4 * N_HEADS


class _PageStream:
    def __init__(self, pt_ref, layer, caches, rings, sem_ref, n_chunks):
        self.pt_ref, self.layer, self.caches, self.rings = pt_ref, layer, caches, rings
        self.sem_ref, self.n_chunks = sem_ref, n_chunks

    def _copies(self, g):
        slot = g % RING_SLOTS
        for r in range(CHUNK_PAGES):
            page = self.pt_ref[g * CHUNK_PAGES + r]
            for which, (cache, ring) in enumerate(zip(self.caches, self.rings)):
                yield pltpu.make_async_copy(cache.at[self.layer, page], ring.at[slot, r],
                                            self.sem_ref.at[slot, which])

    def start(self, g):
        for cp in self._copies(g):
            cp.start()

    def wait(self, g):
        for cp in self._copies(g):
            cp.wait()

    def consume(self, b, chunks_per_seq, use):
        @pl.when(b == 0)
        def _():
            self.start(0)
            self.start(1)

        for k in range(chunks_per_seq):
            g = b * chunks_per_seq + k
            self.wait(g)

            @pl.when(g + 2 < self.n_chunks)
            def _():
                self.start(g + 2)

            use(k, g % RING_SLOTS)


def _ring(cache):
    return pltpu.VMEM((RING_SLOTS, CHUNK_PAGES) + cache.shape[2:], cache.dtype)


def _feature_major(cache):
    l, pool, page = cache.shape[:3]
    perm = (0, 1) + tuple(range(3, cache.ndim)) + (2,)
    return cache.transpose(perm).reshape(l, pool, -1, page)


def _sample_tokens():
    return lax.broadcasted_iota(jnp.int32, (SAMPLE_ROWS, PAGE_SIZE), 0) // N_HEADS


def _own_group_lanes(x):
    rows = x.shape[0]
    lane = lax.broadcasted_iota(jnp.int32, (rows, LANE), 1)
    g = (lax.broadcasted_iota(jnp.int32, (rows, LANE), 0) % N_HEADS) // (N_HEADS // KV_HEADS)
    return jnp.where((lane >= HEAD_DIM * g) & (lane < HEAD_DIM * (g + 1)), x, 0.0)


def _mla_sample_kernel(pt_ref, q_ref, knew_ref, ckv_hbm, kpe_hbm, o_ref, s_ref, kbuf_ref,
                       ckv_ring, kpe_ring, sem_ref, *, n_pages, layer, n_chunks):
    q = q_ref[...]
    q_lat = q[:, :MLA_KV_LORA]
    q_pe = q[:, MLA_KV_LORA:MLA_KV_LORA + MLA_ROPE]

    def use(k, slot):
        for r in range(CHUNK_PAGES):
            pg = k * CHUNK_PAGES + r
            ckv = ckv_ring[slot, r].astype(BF16)
            kpe_t = kpe_ring[slot, r].astype(BF16)
            kbuf_ref[pg] = ckv
            s_ref[pg] = (_dot_nt(q_lat, ckv) + _dot(q_pe, kpe_t)) * MLA_SCALE

    stream = _PageStream(pt_ref, layer, (ckv_hbm, kpe_hbm), (ckv_ring, kpe_ring), sem_ref, n_chunks)
    stream.consume(pl.program_id(0), n_pages // CHUNK_PAGES, use)

    kn = knew_ref[...]
    s_new = _dot_nt(q, kn) * MLA_SCALE
    col = lax.broadcasted_iota(jnp.int32, s_new.shape, 1)
    s_new = jnp.where(col <= _sample_tokens(), s_new, NEG)
    s_all = s_ref[...]
    m = jnp.maximum(jnp.max(jnp.max(s_all, axis=0), axis=-1, keepdims=True),
                    jnp.max(s_new, axis=-1, keepdims=True))
    p_new = jnp.exp(s_new - m)
    p_all = jnp.exp(s_all - m)
    l = (jnp.sum(p_new, axis=-1, keepdims=True)
         + jnp.sum(jnp.sum(p_all, axis=0), axis=-1, keepdims=True))
    acc = _dot(p_new.astype(BF16), kn[:, :MLA_KV_LORA])
    for pg in range(n_pages):
        acc = acc + _dot(p_all[pg].astype(BF16), kbuf_ref[pg])
    o_ref[...] = acc / l


def _sample_call(kernel, name, pt, layer, inputs, in_specs, caches, out_width, scratch_shapes):
    b = inputs[0].shape[0] // SAMPLE_ROWS
    n_pages = pt.shape[0] // b
    assert n_pages % CHUNK_PAGES == 0
    n_chunks = b * n_pages // CHUNK_PAGES
    assert n_chunks >= 2
    grid_spec = pltpu.PrefetchScalarGridSpec(
        num_scalar_prefetch=1, grid=(b,),
        in_specs=in_specs + [pl.BlockSpec(memory_space=pl.ANY)] * len(caches),
        out_specs=_row_spec(out_width),
        scratch_shapes=scratch_shapes + [_ring(c) for c in caches]
        + [pltpu.SemaphoreType.DMA((RING_SLOTS, len(caches)))])
    return pl.pallas_call(
        functools.partial(kernel, n_pages=n_pages, layer=layer, n_chunks=n_chunks),
        grid_spec=grid_spec,
        out_shape=jax.ShapeDtypeStruct((b * SAMPLE_ROWS, out_width), F32),
        compiler_params=_cparams("arbitrary"),
        name=name,
    )(pt, *inputs, *caches)


def _row_spec(width):
    return pl.BlockSpec((SAMPLE_ROWS, width), lambda i, pt: (i, 0))


def _new_spec(width):
    return pl.BlockSpec((None, PAGE_SIZE, width), lambda i, pt: (i, 0, 0))


def _mla_sample(pt, q, knew, cache_ckv, cache_kpe, layer):
    n_pages = pt.shape[0] // (q.shape[0] // SAMPLE_ROWS)
    return _sample_call(
        _mla_sample_kernel, "mla_sample", pt, layer, [q, knew],
        [_row_spec(2 * LANE), _new_spec(2 * LANE)], [cache_ckv, cache_kpe], MLA_KV_LORA,
        [pltpu.VMEM((n_pages, SAMPLE_ROWS, PAGE_SIZE), F32),
         pltpu.VMEM((n_pages, PAGE_SIZE, MLA_KV_LORA), BF16)])


def _sb_sample_kernel(pt_ref, q_ref, knew_ref, vnew_ref, uu_ref, k_hbm, v_hbm, o_ref, z_ref, vbuf_ref,
                      k_ring, v_ring, sem_ref, *, n_pages, layer, n_chunks):
    q = q_ref[...]

    def use(k, slot):
        for r in range(CHUNK_PAGES):
            pg = k * CHUNK_PAGES + r
            z_ref[pg] = _dot(q, k_ring[slot, r].astype(BF16))
            vbuf_ref[pg] = v_ring[slot, r].astype(BF16)

    stream = _PageStream(pt_ref, layer, (k_hbm, v_hbm), (k_ring, v_ring), sem_ref, n_chunks)
    stream.consume(pl.program_id(0), n_pages // CHUNK_PAGES, use)

    uu = uu_ref[...]
    ls, l1m = _log_sigmoid_pair(_dot_nt(q, knew_ref[...]))
    col = lax.broadcasted_iota(jnp.int32, ls.shape, 1)
    mask = col < _sample_tokens()
    l1m = jnp.where(mask, l1m, 0.0)
    within = _suffix_sum(l1m, uu)
    a = jnp.where(mask, jnp.exp(ls + within), 0.0)
    acc = _dot(a.astype(BF16), vnew_ref[...])
    carry = within[:, 0:1] + l1m[:, 0:1]
    ls, l1m = _log_sigmoid_pair(z_ref[...])
    within = _suffix_sum(l1m.reshape(n_pages * SAMPLE_ROWS, PAGE_SIZE), uu)
    within = within.reshape(n_pages, SAMPLE_ROWS, PAGE_SIZE)
    for pg in reversed(range(n_pages)):
        a = jnp.exp(ls[pg] + within[pg] + carry)
        acc = acc + _dot_nt(a.astype(BF16), vbuf_ref[pg])
        carry = carry + within[pg][:, 0:1] + l1m[pg][:, 0:1]
    o_ref[...] = _own_group_lanes(acc)


def _sb_sample(pt, q, knew, vnew, cache_k, cache_v, layer):
    n_pages = pt.shape[0] // (q.shape[0] // SAMPLE_ROWS)
    return _sample_call(
        _sb_sample_kernel, "sb_sample", pt, layer, [q, knew, vnew, _suffix_matrix(PAGE_SIZE)],
        [_row_spec(LANE), _new_spec(LANE), _new_spec(LANE),
         pl.BlockSpec((2 * PAGE_SIZE, PAGE_SIZE), lambda i, pt: (0, 0))],
        [cache_k, cache_v], LANE,
        [pltpu.VMEM((n_pages, SAMPLE_ROWS, PAGE_SIZE), F32),
         pltpu.VMEM((n_pages, LANE, PAGE_SIZE), BF16)])


def _moba_sample_kernel(pt_ref, qh_ref, ql_ref, knew_ref, vnew_ref, bias_ref, far_ref, k_hbm, v_hbm,
                        o_ref, s_ref, vbuf_ref, kmean_ref, k_ring, v_ring, sem_ref,
                        *, n_pages, layer, n_chunks):
    qh = qh_ref[...]
    pages_per_block = MB_BLOCK // PAGE_SIZE
    n_blocks = n_pages // pages_per_block
    kmean_ref[...] = jnp.zeros(kmean_ref.shape, F32)
    blk_lane = lax.broadcasted_iota(jnp.int32, kmean_ref.shape, 1)

    def use(k, slot):
        for r in range(0, CHUNK_PAGES, pages_per_block):
            ksum = jnp.zeros((LANE, 1), F32)
            for rr in range(r, r + pages_per_block):
                pg = k * CHUNK_PAGES + rr
                page = k_ring[slot, rr]
                s_ref[pg] = _dot(qh, page.astype(BF16))
                vbuf_ref[pg] = v_ring[slot, rr].astype(BF16)
                ksum = ksum + jnp.sum(page, axis=1, keepdims=True)
            blk = (k * CHUNK_PAGES + r) // pages_per_block
            kmean_ref[...] += jnp.where(blk_lane == blk, ksum * (1.0 / MB_BLOCK), 0.0)

    stream = _PageStream(pt_ref, layer, (k_hbm, v_hbm), (k_ring, v_ring), sem_ref, n_chunks)
    stream.consume(pl.program_id(0), n_pages // CHUNK_PAGES, use)

    kh, kl = _split_bf16(kmean_ref[...])
    score = _dot(qh, kh) + _dot(qh, kl) + _dot(ql_ref[...], kh)
    blk = lax.broadcasted_iota(jnp.int32, score.shape, 1)
    sel = _top3_select(score, blk < n_blocks)
    far_bias = far_ref[...]
    bias = bias_ref[...]
    s_new = _dot_nt(qh, knew_ref[...]) + bias[:, MB_BLOCK:MB_BLOCK + PAGE_SIZE]
    col = lax.broadcasted_iota(jnp.int32, s_new.shape, 1)
    s_new = jnp.where(col <= _sample_tokens(), s_new, NEG)
    m_lanes = s_new
    masked = []
    for n in range(n_blocks):
        keep = sel[:, n:n + 1] > 0.0
        for pg in range(n * pages_per_block, (n + 1) * pages_per_block):
            if n == n_blocks - 1:
                off = (pg - n * pages_per_block) * PAGE_SIZE
                s = s_ref[pg] + bias[:, off:off + PAGE_SIZE]
            else:
                s = s_ref[pg] + far_bias
            s = jnp.where(keep, s, NEG)
            masked.append(s)
            m_lanes = jnp.maximum(m_lanes, s)
    m = jnp.max(m_lanes, axis=-1, keepdims=True)
    p_new = jnp.exp(s_new - m)
    l_lanes = p_new
    acc = _dot(p_new.astype(BF16), vnew_ref[...])
    for pg in range(n_pages):
        p = jnp.exp(masked[pg] - m)
        l_lanes = l_lanes + p
        acc = acc + _dot_nt(p.astype(BF16), vbuf_ref[pg])
    o_ref[...] = _own_group_lanes(acc / jnp.sum(l_lanes, axis=-1, keepdims=True))


def _moba_sample(pt, qh, ql, knew, vnew, bias, far_bias, cache_k, cache_v, layer, nbp):
    n_pages = pt.shape[0] // (qh.shape[0] // SAMPLE_ROWS)
    return _sample_call(
        _moba_sample_kernel, "moba_sample", pt, layer, [qh, ql, knew, vnew, bias, far_bias],
        [_row_spec(LANE), _row_spec(LANE), _new_spec(LANE), _new_spec(LANE),
         pl.BlockSpec(bias.shape, lambda i, pt: (0, 0)), pl.BlockSpec(far_bias.shape, lambda i, pt: (0, 0))],
        [cache_k, cache_v], LANE,
        [pltpu.VMEM((n_pages, SAMPLE_ROWS, PAGE_SIZE), F32),
         pltpu.VMEM((n_pages, LANE, PAGE_SIZE), BF16),
         pltpu.VMEM((LANE, nbp), F32)])


CONV_HALO = 32


def _conv_prompt_kernel(cur_ref, prev_ref, w_ref, b_ref, g_ref, beta_ref, o_ref, ext_ref, *, tm):
    i = pl.program_id(0)
    prev = prev_ref[...]
    ext_ref[:CONV_HALO, :] = jnp.where(i > 0, prev, 0.0)
    ext_ref[CONV_HALO:, :] = cur_ref[...]
    acc = jnp.zeros((tm, CONV_CH), F32)
    base = CONV_HALO - (CONV_WIDTH - 1)
    for w in range(CONV_WIDTH):
        acc = acc + ext_ref[base + w:base + w + tm, :] * w_ref[w:w + 1, :]
    o_ref[...] = _silu(_layer_norm(acc + b_ref[...], g_ref[...], beta_ref[...]))


def _conv_prompt(glu, t_p, conv_w, conv_b, ln_g, ln_b):
    tm = _pick_tile(t_p, (512, 256, 128, 64, 32))
    per = tm // CONV_HALO
    return pl.pallas_call(
        functools.partial(_conv_prompt_kernel, tm=tm),
        grid=(t_p // tm,),
        in_specs=[pl.BlockSpec((tm, CONV_CH), lambda i: (i, 0)),
                  pl.BlockSpec((CONV_HALO, CONV_CH), lambda i: (jnp.maximum(i * per - 1, 0), 0)),
                  _full(conv_w.shape), _full(conv_b.shape), _full(ln_g.shape), _full(ln_b.shape)],
        out_specs=pl.BlockSpec((tm, CONV_CH), lambda i: (i, 0)),
        out_shape=jax.ShapeDtypeStruct((t_p, CONV_CH), F32),
        scratch_shapes=[pltpu.VMEM((tm + CONV_HALO, CONV_CH), F32)],
        compiler_params=_cparams("parallel"),
        name="conv_prompt",
    )(glu, glu, conv_w, conv_b, ln_g, ln_b)


def _conv_sample_kernel(ext_ref, w_ref, b_ref, g_ref, beta_ref, o_ref):
    n_new = o_ref.shape[0]
    for t in range(n_new):
        acc = jnp.zeros(o_ref.shape[1:], F32)
        for w in range(CONV_WIDTH):
            acc = acc + ext_ref[t + w] * w_ref[w:w + 1, :]
        o_ref[t] = _silu(_layer_norm(acc + b_ref[...], g_ref[...], beta_ref[...]))


def _conv_sample(ext_t, conv_w, conv_b, ln_g, ln_b):
    n_new = ext_t.shape[0] - (CONV_WIDTH - 1)
    vm = pl.BlockSpec(memory_space=pltpu.VMEM)
    return pl.pallas_call(
        _conv_sample_kernel,
        in_specs=[vm] * 5,
        out_specs=vm,
        out_shape=jax.ShapeDtypeStruct((n_new,) + ext_t.shape[1:], F32),
        name="conv_sample",
    )(ext_t, conv_w, conv_b, ln_g, ln_b)


def _merge_kernel(x_ref, cact_ref, lat_ref, sb_ref, mb_ref, gate_ref, gb_ref, wc_ref, wuv_ref, wm_ref,
                  ws_ref, wo_ref, wout_ref, g_ref, b_ref, o_ref, *, alpha):
    c = _dot(cact_ref[...].astype(BF16), wc_ref[...])
    mv = _dot(lat_ref[...].astype(BF16), wuv_ref[...])
    m = _dot(mv.astype(BF16), wm_ref[...])
    s = _dot(sb_ref[...].astype(BF16), ws_ref[...])
    o = _dot(mb_ref[...].astype(BF16), wo_ref[...])
    g = jax.nn.sigmoid(gate_ref[...] + gb_ref[...])
    d = D_MODEL
    y = g[:, :d] * c + g[:, d:2 * d] * m + g[:, 2 * d:3 * d] * s + g[:, 3 * d:] * o
    y = _dot(y.astype(BF16), wout_ref[...])
    o_ref[...] = _layer_norm(alpha * x_ref[...] + y, g_ref[...], b_ref[...])


def _merge(x, cact, lat, sb, mb, gate, gate_b, wc, wuv, wm, ws, wo, wout, ln_g, ln_b, alpha):
    n = x.shape[0]
    tm = _pick_tile(n, (256, 128, 64, 32, 16, 8))
    row = lambda w: pl.BlockSpec((tm, w), lambda i: (i, 0))
    weights = (gate_b, wc, wuv, wm, ws, wo, wout, ln_g, ln_b)
    return pl.pallas_call(
        functools.partial(_merge_kernel, alpha=alpha),
        grid=(n // tm,),
        in_specs=[row(a.shape[1]) for a in (x, cact, lat, sb, mb, gate)] + [_full(w.shape) for w in weights],
        out_specs=row(D_MODEL),
        out_shape=jax.ShapeDtypeStruct((n, D_MODEL), F32),
        compiler_params=_cparams("parallel"),
        name="merge",
    )(x, cact, lat, sb, mb, gate, *weights)


EXPERTS_PER_STEP = 4


def _first_index_of_max(s, idx):
    mx = jnp.max(s, axis=0, keepdims=True)
    first = jnp.min(jnp.where(s == mx, idx, 1e9), axis=0, keepdims=True)
    return idx == first


def _pick_column(x, j):
    lane = lax.broadcasted_iota(jnp.int32, x.shape, 1)
    return jnp.sum(jnp.where(lane == j, x, 0.0), axis=-1, keepdims=True)


def _router_kernel(x_ref, wh_ref, wl_ref, bias_ref, gate_ref, xb_ref):
    x = x_ref[...]
    xh, xl = _split_bf16(x)
    xb_ref[...] = xh
    logits = (_dot_nt(wh_ref[...], xh) + _dot_nt(wl_ref[...], xh) + _dot_nt(wh_ref[...], xl))[:N_EXPERTS]
    scores = jax.nn.sigmoid(logits)
    biased = scores + bias_ref[...]
    e, tm = biased.shape
    per = e // N_GROUPS
    grp = biased.reshape(N_GROUPS, per, tm)
    sub = lax.broadcasted_iota(jnp.int32, grp.shape, 1).astype(F32)
    top1 = jnp.max(grp, axis=1, keepdims=True)
    first = jnp.min(jnp.where(grp == top1, sub, 1e9), axis=1, keepdims=True)
    top2 = jnp.max(jnp.where(sub == first, -jnp.inf, grp), axis=1, keepdims=True)
    gs = jnp.broadcast_to(top1 + top2, grp.shape).reshape(e, tm)
    eidx = lax.broadcasted_iota(jnp.int32, biased.shape, 0)
    gidx = (eidx // per).astype(F32)
    e_keep = jnp.zeros(biased.shape, F32)
    for _ in range(TOPK_GROUPS):
        pick = _first_index_of_max(gs, gidx)
        e_keep = jnp.where(pick, 1.0, e_keep)
        gs = jnp.where(pick, -jnp.inf, gs)
    eidx = eidx.astype(F32)
    cand = jnp.where(e_keep > 0.0, biased, -jnp.inf)
    chosen = jnp.zeros(biased.shape, F32)
    for _ in range(TOP_K):
        pick = _first_index_of_max(cand, eidx)
        chosen = jnp.where(pick, 1.0, chosen)
        cand = jnp.where(pick, -jnp.inf, cand)
    sel = chosen * scores
    wts = sel / jnp.sum(sel, axis=0, keepdims=True) * ROUTED_SCALE
    gate_ref[...] = jnp.concatenate([wts, jnp.zeros_like(wts)], axis=0).T


def _router(x, router_w, router_bias):
    n = x.shape[0]
    tm = _pick_tile(n, (512, 256, 128))
    wt = jnp.pad(router_w.T, ((0, LANE - N_EXPERTS), (0, 0)))
    wh, wl = _split_bf16(wt)
    return pl.pallas_call(
        _router_kernel,
        grid=(n // tm,),
        in_specs=[pl.BlockSpec((tm, D_MODEL), lambda i: (i, 0)), _full(wh.shape), _full(wl.shape),
                  _full((N_EXPERTS, 1))],
        out_specs=[pl.BlockSpec((tm, LANE), lambda i: (i, 0)),
                   pl.BlockSpec((tm, D_MODEL), lambda i: (i, 0))],
        out_shape=[jax.ShapeDtypeStruct((n, LANE), F32), jax.ShapeDtypeStruct((n, D_MODEL), BF16)],
        compiler_params=_cparams("parallel"),
        name="moe_router",
    )(x, wh, wl, router_bias.reshape(N_EXPERTS, 1))


def _experts_kernel(x_ref, xb_ref, gate_ref, wg_ref, wu_ref, wd_ref, wsg_ref, wsu_ref, wsd_ref,
                    g_ref, b_ref, o_ref, acc_ref, *, alpha):
    step = pl.program_id(1)
    xb = xb_ref[...]

    @pl.when(step == 0)
    def _():
        h = _silu(_dot(xb, wsg_ref[...])) * _dot(xb, wsu_ref[...])
        acc_ref[...] = _dot(h.astype(BF16), wsd_ref[...])

    gate = gate_ref[...]
    hs = []
    for k in range(EXPERTS_PER_STEP):
        w = _pick_column(gate, step * EXPERTS_PER_STEP + k)
        h = _silu(_dot(xb, wg_ref[k])) * _dot(xb, wu_ref[k]) * w
        hs.append(h.astype(BF16))
    acc_ref[...] += _dot(jnp.concatenate(hs, axis=-1), wd_ref[...])

    @pl.when(step == pl.num_programs(1) - 1)
    def _():
        o_ref[...] = _layer_norm(alpha * x_ref[...] + acc_ref[...], g_ref[...], b_ref[...])


def _experts(x, xb, gate, wg, wu, wd, wsg, wsu, wsd, ln_g, ln_b, alpha):
    n = x.shape[0]
    tm = _pick_tile(n, (768, 512, 256, 128))
    n_exp, ff = wg.shape[0], wg.shape[2]
    wd = wd.reshape(n_exp // EXPERTS_PER_STEP, EXPERTS_PER_STEP * ff, D_MODEL)
    row = lambda w: pl.BlockSpec((tm, w), lambda i, e: (i, 0))
    return pl.pallas_call(
        functools.partial(_experts_kernel, alpha=alpha),
        grid=(n // tm, n_exp // EXPERTS_PER_STEP),
        in_specs=[row(D_MODEL), row(D_MODEL), row(LANE),
                  pl.BlockSpec((EXPERTS_PER_STEP, D_MODEL, ff), lambda i, e: (e, 0, 0)),
                  pl.BlockSpec((EXPERTS_PER_STEP, D_MODEL, ff), lambda i, e: (e, 0, 0)),
                  pl.BlockSpec((None, EXPERTS_PER_STEP * ff, D_MODEL), lambda i, e: (e, 0, 0)),
                  _full(wsg.shape), _full(wsu.shape), _full(wsd.shape), _full(ln_g.shape), _full(ln_b.shape)],
        out_specs=row(D_MODEL),
        out_shape=jax.ShapeDtypeStruct((n, D_MODEL), F32),
        scratch_shapes=[pltpu.VMEM((tm, D_MODEL), F32)],
        compiler_params=_cparams("parallel", "arbitrary"),
        name="moe_experts",
    )(x, xb, gate, wg, wu, wd, wsg, wsu, wsd, ln_g, ln_b)


def _vec(v):
    return v.reshape(1, -1)


def _pad_new(a):
    return jnp.pad(a, ((0, 0), (0, PAGE_SIZE - a.shape[1]), (0, 0))).astype(BF16)


def kernel(x_prompt, x_sample, cache_mla_ckv, cache_mla_kpe, cache_sb_k, cache_sb_v, cache_moba_k, cache_moba_v, state_conv, page_table, t5_bias, w_in, gate_b, conv_w, conv_b, conv_ln_g, conv_ln_b, w_conv_out, mla_q_norm, w_q_up, mla_kv_norm, w_kv_up, w_mla_out, w_sb_out, w_mb_out, w_out, ln1_g, ln1_b, router_w, router_bias, w_e_gate, w_e_up, w_e_down, w_s_gate, w_s_up, w_s_down, ln2_g, ln2_b):
    depth = w_in.shape[0]
    alpha = (2 * depth) ** 0.25
    bp, t_p, _ = x_prompt.shape
    assert bp == 1
    bs, n_new, _ = x_sample.shape
    n_s = bs * n_new
    n_pages = page_table.shape[1]
    past = n_pages * PAGE_SIZE
    assert n_new * N_HEADS == SAMPLE_ROWS
    assert t_p % MB_BLOCK == 0 and past % MB_BLOCK == 0

    x = jnp.concatenate([x_prompt.reshape(t_p, D_MODEL), x_sample.reshape(n_s, D_MODEL)], axis=0)
    pos = jnp.concatenate([jnp.arange(t_p, dtype=jnp.int32),
                           jnp.tile(past + jnp.arange(n_new, dtype=jnp.int32), bs)])
    cos, sin = _rope_tables(pos)
    pt = page_table.reshape(-1)
    kpe_t, sbk_t, sbv_t, mbk_t, mbv_t = (
        _feature_major(c) for c in (cache_mla_kpe, cache_sb_k, cache_sb_v, cache_moba_k, cache_moba_v))

    tq = MB_BLOCK
    rel_p = jnp.arange(tq)[:, None] - jnp.arange(2 * tq)[None, :] + tq
    bias_p = _t5_tiles(t5_bias, _t5_bucket(rel_p).astype(jnp.int32)).reshape(N_HEADS * tq, 2 * tq)
    kpos_s = jnp.concatenate([past - MB_BLOCK + jnp.arange(MB_BLOCK), past + jnp.arange(PAGE_SIZE)])
    rel_s = past + jnp.arange(8)[:, None] - kpos_s[None, :]
    bias_s = _t5_tiles(t5_bias, _t5_bucket(rel_s).astype(jnp.int32))
    bias_s = bias_s[:, :n_new].transpose(1, 0, 2).reshape(SAMPLE_ROWS, MB_BLOCK + PAGE_SIZE)
    far_s = jnp.tile(t5_bias[T5_BUCKETS - 1], n_new).reshape(SAMPLE_ROWS, 1)

    nb_p = t_p // MB_BLOCK
    nb_s = past // MB_BLOCK
    nbp_p = -(-nb_p // LANE) * LANE
    nbp_s = -(-nb_s // LANE) * LANE

    new_p = {k: [] for k in ("ckv", "kpe", "sk", "sv", "mk", "mv", "conv")}
    new_s = {k: [] for k in new_p}
    for l in range(depth):
        w1, wg, wq, wuk, wuv = _pack_layer_weights(w_in[l], w_q_up[l], w_kv_up[l])
        pc = _project(x, cos, sin, w1, wg, wq, wuk, _vec(mla_q_norm[l]), _vec(mla_kv_norm[l]), t_p)
        for k in ("ckv", "kpe", "sk", "sv", "mk", "mv"):
            new_p[k].append(pc[k][:t_p])
            new_s[k].append(pc[k][t_p:])

        lat_p = _mla_prompt(pc["qcat"], pc["kcat"], t_p, _pick_tile(t_p, (256, 128)))
        sb_p = _sb_prompt(pc["sq"], pc["skv"], t_p, _pick_tile(t_p, (256, 128)))
        kmean = jnp.pad(_block_means(pc["mk"], t_p), ((0, nbp_p - nb_p), (0, 0)))
        mb_p = _moba_prompt(t5_bias, pc["mqh"], pc["mql"], pc["mkx"], pc["mvb"], kmean, bias_p, t_p)
        glu_p = pc["glu"][:t_p]
        cact_p = _conv_prompt(pc["glu"], t_p, conv_w[l], _vec(conv_b[l]), _vec(conv_ln_g[l]), _vec(conv_ln_b[l]))
        new_p["conv"].append(glu_p[t_p - (CONV_WIDTH - 1):])

        def rows(a, width):
            return a[t_p:].reshape(n_s * N_HEADS, width)

        def newkeys(a):
            return _pad_new(a[t_p:].reshape(bs, n_new, a.shape[1]))

        lat_s = _mla_sample(pt, rows(pc["qcat"], 2 * LANE), newkeys(pc["kcat"]),
                            cache_mla_ckv, kpe_t, l)
        skv_new = newkeys(pc["skv"])
        sb_s = _sb_sample(pt, rows(pc["sq"], LANE), skv_new[..., :LANE], skv_new[..., LANE:], sbk_t, sbv_t, l)
        mb_s = _moba_sample(pt, rows(pc["mqh"], LANE), rows(pc["mql"], LANE),
                            newkeys(pc["mkx"])[..., :LANE], newkeys(pc["mvb"]), bias_s, far_s,
                            mbk_t, mbv_t, l, nbp_s)
        ext_t = jnp.concatenate([state_conv[l].transpose(1, 0, 2),
                                 pc["glu"][t_p:].reshape(bs, n_new, CONV_CH).transpose(1, 0, 2)], axis=0)
        new_s["conv"].append(ext_t[n_new:].transpose(1, 0, 2))
        cact_s = _conv_sample(ext_t, conv_w[l], _vec(conv_b[l]), _vec(conv_ln_g[l]), _vec(conv_ln_b[l]))
        cact_s = cact_s.transpose(1, 0, 2).reshape(n_s, CONV_CH)

        cact = jnp.concatenate([cact_p, cact_s], axis=0)
        lat = jnp.concatenate([lat_p, lat_s.reshape(n_s, MLA_HEADS * MLA_KV_LORA)], axis=0)
        sb = jnp.concatenate([sb_p, sb_s.reshape(n_s, N_HEADS * LANE)], axis=0)
        mb = jnp.concatenate([mb_p, mb_s.reshape(n_s, N_HEADS * LANE)], axis=0)
        x1 = _merge(x, cact, lat, sb, mb, pc["gate"], _vec(gate_b[l]),
                    w_conv_out[l].astype(BF16), wuv, w_mla_out[l].astype(BF16),
                    _head_pad_cols(w_sb_out[l].T).T.astype(BF16), _head_pad_cols(w_mb_out[l].T).T.astype(BF16),
                    w_out[l].astype(BF16), _vec(ln1_g[l]), _vec(ln1_b[l]), alpha)
        gate, xb = _router(x1, router_w[l], router_bias[l])
        x = _experts(x1, xb, gate, w_e_gate[l].astype(BF16), w_e_up[l].astype(BF16), w_e_down[l].astype(BF16),
                     w_s_gate[l].astype(BF16), w_s_up[l].astype(BF16), w_s_down[l].astype(BF16),
                     _vec(ln2_g[l]), _vec(ln2_b[l]), alpha)

    def stack_p(k, tail):
        return jnp.stack(new_p[k]).reshape((depth, 1, t_p) + tail)

    def stack_s(k, tail):
        return jnp.stack(new_s[k]).reshape((depth, bs, n_new) + tail)

    kv = (KV_HEADS, HEAD_DIM)
    return (x[:t_p].reshape(1, t_p, D_MODEL), x[t_p:].reshape(bs, n_new, D_MODEL),
            stack_p("ckv", (MLA_KV_LORA,)), stack_s("ckv", (MLA_KV_LORA,)),
            stack_p("kpe", (MLA_ROPE,)), stack_s("kpe", (MLA_ROPE,)),
            stack_p("sk", kv), stack_s("sk", kv), stack_p("sv", kv), stack_s("sv", kv),
            stack_p("mk", kv), stack_s("mk", kv), stack_p("mv", kv), stack_s("mv", kv),
            jnp.stack(new_p["conv"]).reshape(depth, 1, CONV_WIDTH - 1, CONV_CH),
            jnp.stack(new_s["conv"]))
```

```python
import functools
import math

import numpy as np
import jax
import jax.numpy as jnp
from jax import lax
from jax.experimental import pallas as pl
from jax.experimental.pallas import tpu as pltpu

F32 = jnp.float32
BF16 = jnp.bfloat16

D_MODEL = 1024
HEAD_DIM = 64
CONV_CH = 256
CONV_WIDTH = 31
MLA_HEADS = 4
MLA_Q_LORA = 256
MLA_KV_LORA = 128
MLA_NOPE = 64
MLA_ROPE = 32
MLA_V = 64
MLA_SCALE = (MLA_NOPE + MLA_ROPE) ** -0.5
ROPE_THETA = 10000.0
N_HEADS = 4
KV_HEADS = 2
QK_SCALE = HEAD_DIM ** -0.5
MB_BLOCK = 256
MB_TOPK = 3
T5_BUCKETS = 32
T5_MAX_DIST = 128
N_BRANCH = 4
N_EXPERTS = 64
EXPERT_FF = 256
SHARED_FF = 256
TOP_K = 8
N_GROUPS = 8
TOPK_GROUPS = 4
ROUTED_SCALE = 2.5
LN_EPS = 1e-5
PAGE_SIZE = 128

LANE = 128
HALF_LANE = LANE // 2
NEG = -1e30
LOG2_E = math.log2(math.e)
VMEM_LIMIT = 56 * 1024 * 1024

NT_DIMS = (((1,), (1,)), ((), ()))


def _cparams(*sem):
    return pltpu.CompilerParams(dimension_semantics=sem, vmem_limit_bytes=VMEM_LIMIT)


def _pick_tile(n, cands):
    for c in cands:
        if n % c == 0:
            return c
    raise ValueError(f"no tile for {n}")


def _full(shape):
    nd = len(shape)
    return pl.BlockSpec(shape, lambda *_: (0,) * nd)


def _dot(a, b):
    return jnp.dot(a, b, preferred_element_type=F32)


def _dot_nt(a, b):
    return lax.dot_general(a, b, NT_DIMS, preferred_element_type=F32)


def _split_bf16(x):
    hi = x.astype(BF16)
    lo = (x - hi.astype(F32)).astype(BF16)
    return hi, lo


def _layer_norm(x, g, b):
    mu = jnp.mean(x, axis=-1, keepdims=True)
    d = x - mu
    var = jnp.mean(d * d, axis=-1, keepdims=True)
    return d * lax.rsqrt(var + LN_EPS) * g + b


def _rms_norm(x, g):
    return x * lax.rsqrt(jnp.mean(x * x, axis=-1, keepdims=True) + LN_EPS) * g


def _silu(x):
    return x * jax.nn.sigmoid(x)


def _head_pad_cols(w):
    k = w.shape[0]
    wh = w.reshape(k, N_HEADS, HEAD_DIM)
    z = jnp.zeros_like(wh)
    per_head = [jnp.concatenate([wh[:, h], z[:, h]] if h // 2 == 0 else [z[:, h], wh[:, h]], axis=-1)
                for h in range(N_HEADS)]
    return jnp.concatenate(per_head, axis=-1)


def _rot_cols(w):
    half = w.shape[-1] // 2
    return jnp.concatenate([-w[..., half:], w[..., :half]], axis=-1)


def _pad_cols(w, width):
    return jnp.pad(w, ((0, 0), (0, width - w.shape[-1])))


C_CONV, C_CQ, C_CKV, C_SQ, C_SK, C_SV, C_MQ, C_MK, C_MV, C_KPE, C_KROT, C_END = (
    0, 512, 768, 896, 1408, 1536, 1664, 2176, 2304, 2432, 2560, 2688)


def _pack_layer_weights(w_in, w_q_up, w_kv_up):
    pts = np.cumsum((2 * CONV_CH, MLA_Q_LORA, MLA_KV_LORA, MLA_ROPE, 256, 128, 128, 256, 128, 128))
    cv, cq, ckv, kpe, sq, sk, sv, mq, mk, mv, gate = jnp.split(w_in, pts.tolist(), axis=-1)
    w1 = jnp.concatenate([
        cv, cq, ckv, _head_pad_cols(sq), sk, sv, _head_pad_cols(mq), mk, mv,
        _pad_cols(kpe, LANE), _pad_cols(_rot_cols(kpe), LANE)], axis=-1).astype(BF16)
    wq = w_q_up.reshape(MLA_Q_LORA, MLA_HEADS, MLA_NOPE + MLA_ROPE)
    nope = wq[:, :, :MLA_NOPE].reshape(MLA_Q_LORA, MLA_HEADS * MLA_NOPE)
    pe = [_pad_cols(wq[:, h, MLA_NOPE:], LANE) for h in range(MLA_HEADS)]
    rot = [_pad_cols(_rot_cols(wq[:, h, MLA_NOPE:]), LANE) for h in range(MLA_HEADS)]
    wq_p = jnp.concatenate([nope] + pe + rot, axis=-1).astype(BF16)
    wkv = w_kv_up.reshape(MLA_KV_LORA, MLA_HEADS, MLA_NOPE + MLA_V)
    wuk = jnp.zeros((MLA_HEADS * MLA_NOPE, MLA_HEADS * MLA_KV_LORA), F32)
    wuv = jnp.zeros((MLA_HEADS * MLA_KV_LORA, MLA_HEADS * MLA_V), F32)
    for h in range(MLA_HEADS):
        wuk = wuk.at[h * MLA_NOPE:(h + 1) * MLA_NOPE, h * MLA_KV_LORA:(h + 1) * MLA_KV_LORA].set(
            wkv[:, h, :MLA_NOPE].T)
        wuv = wuv.at[h * MLA_KV_LORA:(h + 1) * MLA_KV_LORA, h * MLA_V:(h + 1) * MLA_V].set(
            wkv[:, h, MLA_NOPE:])
    return w1, gate.astype(BF16), wq_p, wuk.astype(BF16), wuv.astype(BF16)


def _rope_tables(pos):
    half = MLA_ROPE // 2
    inv = ROPE_THETA ** (-jnp.arange(half, dtype=F32) / half)
    ang = pos.astype(F32)[:, None] * inv[None, :]
    cos = jnp.tile(jnp.cos(ang), (1, 2 * MLA_HEADS))
    sin = jnp.tile(jnp.sin(ang), (1, 2 * MLA_HEADS))
    return cos, sin


def _t5_bucket(rel):
    n = jnp.maximum(rel, 0)
    exact = T5_BUCKETS // 2
    nf = jnp.maximum(n, 1).astype(F32)
    large = exact + (jnp.log(nf / exact) / math.log(T5_MAX_DIST / exact)
                     * (T5_BUCKETS - exact)).astype(jnp.int32)
    large = jnp.minimum(large, T5_BUCKETS - 1)
    return jnp.where(n < exact, n, large)


def _proj_kernel(x_ref, cos_ref, sin_ref, w1_ref, wg_ref, wq_ref, wuk_ref, qn_ref, kvn_ref,
                 glu_ref, ckv_ref, kpe_ref, kcat_ref, qcat_ref, sq_ref, sk_ref, sv_ref, skv_ref,
                 mqh_ref, mql_ref, mk_ref, mv_ref, mkx_ref, mvb_ref, gate_ref, *, tm, t_p):
    x = x_ref[...].astype(BF16)
    u = _dot(x, w1_ref[...])
    gate_ref[...] = _dot(x, wg_ref[...])
    cos = cos_ref[...]
    sin = sin_ref[...]
    glu_ref[...] = u[:, C_CONV:C_CONV + CONV_CH] * jax.nn.sigmoid(u[:, C_CONV + CONV_CH:C_CQ])
    ckvn = _rms_norm(u[:, C_CKV:C_SQ], kvn_ref[...])
    ckv_ref[...] = ckvn
    kpe = u[:, C_KPE:C_KROT] * cos + u[:, C_KROT:C_END] * sin
    kpe_ref[...] = kpe[:, :MLA_ROPE]
    kcat_ref[...] = jnp.concatenate([ckvn, kpe], axis=-1).astype(BF16)
    cqn = _rms_norm(u[:, C_CQ:C_CKV], qn_ref[...])
    qa = _dot(cqn.astype(BF16), wq_ref[...])
    nq = MLA_HEADS * MLA_NOPE
    qlat = _dot(qa[:, :nq].astype(BF16), wuk_ref[...])
    parts = []
    for h in range(MLA_HEADS):
        pe = (qa[:, nq + LANE * h:nq + LANE * (h + 1)] * cos
              + qa[:, nq + LANE * (MLA_HEADS + h):nq + LANE * (MLA_HEADS + h + 1)] * sin)
        parts += [qlat[:, MLA_KV_LORA * h:MLA_KV_LORA * (h + 1)], pe]
    qcat_ref[...] = jnp.concatenate(parts, axis=-1).astype(BF16)
    sq_ref[...] = (u[:, C_SQ:C_SK] * QK_SCALE).astype(BF16)
    sk = u[:, C_SK:C_SV]
    sv = u[:, C_SV:C_MQ]
    sk_ref[...] = sk
    sv_ref[...] = sv
    skv_ref[...] = jnp.concatenate([sk, sv], axis=-1).astype(BF16)
    mqh, mql = _split_bf16(u[:, C_MQ:C_MK] * QK_SCALE)
    mqh_ref[...] = mqh
    mql_ref[...] = mql
    mk = u[:, C_MK:C_MV]
    mv = u[:, C_MV:C_KPE]
    mk_ref[...] = mk
    mv_ref[...] = mv
    mvb_ref[...] = mv.astype(BF16)
    row = pl.program_id(0) * tm + lax.broadcasted_iota(jnp.int32, (tm, LANE), 0)
    lane = lax.broadcasted_iota(jnp.int32, (tm, LANE), 1)
    blk = row // MB_BLOCK
    onehot = jnp.where(lane == blk, 1.0, jnp.where(lane == blk + HALF_LANE, 1.0, 0.0))
    onehot = jnp.where(row < t_p, onehot, 0.0)
    mkx_ref[...] = jnp.concatenate([mk, onehot], axis=-1).astype(BF16)


def _project(x, cos, sin, w1, wg, wq, wuk, qn, kvn, t_p):
    n = x.shape[0]
    tm = _pick_tile(n, (256, 128, 64, 32, 16, 8))
    row = lambda w: pl.BlockSpec((tm, w), lambda i: (i, 0))
    outs = [("glu", CONV_CH, F32), ("ckv", MLA_KV_LORA, F32), ("kpe", MLA_ROPE, F32),
            ("kcat", 2 * LANE, BF16), ("qcat", MLA_HEADS * 2 * LANE, BF16),
            ("sq", N_HEADS * LANE, BF16), ("sk", LANE, F32), ("sv", LANE, F32), ("skv", 2 * LANE, BF16),
            ("mqh", N_HEADS * LANE, BF16), ("mql", N_HEADS * LANE, BF16),
            ("mk", LANE, F32), ("mv", LANE, F32), ("mkx", 2 * LANE, BF16), ("mvb", LANE, BF16),
            ("gate", N_BRANCH * D_MODEL, F32)]
    res = pl.pallas_call(
        functools.partial(_proj_kernel, tm=tm, t_p=t_p),
        grid=(n // tm,),
        in_specs=[row(D_MODEL), row(LANE), row(LANE), _full(w1.shape), _full(wg.shape), _full(wq.shape),
                  _full(wuk.shape), _full(qn.shape), _full(kvn.shape)],
        out_specs=[row(w) for _, w, _ in outs],
        out_shape=[jax.ShapeDtypeStruct((n, w), dt) for _, w, dt in outs],
        compiler_params=_cparams("parallel"),
        name="proj",
    )(x, cos, sin, w1, wg, wq, wuk, qn, kvn)
    return {name: r for (name, _, _), r in zip(outs, res)}


def _causal_ids(tq):
    row = lax.broadcasted_iota(jnp.int32, (tq, tq), 0)
    col = lax.broadcasted_iota(jnp.int32, (tq, tq), 1)
    return row, col


def _flash_init(m_ref, acc_ref):
    m_ref[...] = jnp.full(m_ref.shape, NEG, F32)
    acc_ref[...] = jnp.zeros(acc_ref.shape, F32)


def _flash_update(h, s, v_ones, m_ref, acc_ref, base2=False):
    ex = jnp.exp2 if base2 else jnp.exp
    m_prev = m_ref[h]
    m_new = jnp.maximum(m_prev, jnp.max(s, axis=-1, keepdims=True))
    p = ex(s - jnp.tile(m_new, (1, s.shape[1] // LANE)))
    alpha = ex(m_prev - m_new)
    acc_ref[h] = jnp.tile(alpha, (1, 2)) * acc_ref[h] + _dot(p.astype(BF16), v_ones)
    m_ref[h] = m_new


BLOCKS_PER_TRIP = 4


def _unrolled_blocks(n, step):
    u = BLOCKS_PER_TRIP

    def body(t, carry):
        for k in range(u):
            step(u * t + k)
        return carry

    lax.fori_loop(0, n // u, body, 0)
    for k in range(u - 1):
        @pl.when(n % u > k)
        def _():
            step((n // u) * u + k)


def _flash_result(h, acc_ref):
    acc = acc_ref[h]
    return acc[:, :LANE] / acc[:, LANE:]


def _mla_prompt_kernel(q_ref, k_ref, o_ref, m_ref, acc_ref, *, tq):
    i = pl.program_id(0)
    _flash_init(m_ref, acc_ref)
    ones = jnp.ones((tq, LANE), BF16)

    def step(j, masked):
        k = k_ref[pl.ds(pl.multiple_of(j * tq, tq), tq), :]
        v_ones = jnp.concatenate([k[:, :MLA_KV_LORA], ones], axis=-1)
        for h in range(MLA_HEADS):
            s = _dot_nt(q_ref[:, h * 2 * LANE:(h + 1) * 2 * LANE], k) * (MLA_SCALE * LOG2_E)
            if masked:
                row, col = _causal_ids(tq)
                s = jnp.where(col <= row, s, NEG)
            _flash_update(h, s, v_ones, m_ref, acc_ref, base2=True)

    _unrolled_blocks(i, lambda j: step(j, False))
    step(i, True)
    for h in range(MLA_HEADS):
        o_ref[:, h * MLA_KV_LORA:(h + 1) * MLA_KV_LORA] = _flash_result(h, acc_ref)


def _mla_prompt(qcat, kcat, t_p, tq):
    return pl.pallas_call(
        functools.partial(_mla_prompt_kernel, tq=tq),
        grid=(t_p // tq,),
        in_specs=[pl.BlockSpec((tq, MLA_HEADS * 2 * LANE), lambda i: (i, 0)),
                  pl.BlockSpec((t_p, 2 * LANE), lambda i: (0, 0))],
        out_specs=pl.BlockSpec((tq, MLA_HEADS * MLA_KV_LORA), lambda i: (i, 0)),
        out_shape=jax.ShapeDtypeStruct((t_p, MLA_HEADS * MLA_KV_LORA), F32),
        scratch_shapes=[pltpu.VMEM((MLA_HEADS, tq, LANE), F32), pltpu.VMEM((MLA_HEADS, tq, 2 * LANE), F32)],
        compiler_params=_cparams("parallel"),
        name="mla_prompt",
    )(qcat, kcat)


def _log_sigmoid_pair(z):
    ls = jnp.minimum(z, 0.0) - jnp.log(1.0 + jnp.exp(-jnp.abs(z)))
    return ls, ls - z


def _suffix_sum(l1m, uu):
    hi, lo = _split_bf16(l1m)
    return _dot(jnp.concatenate([hi, lo], axis=-1), uu)


def _group_lanes(x, h):
    lane = lax.broadcasted_iota(jnp.int32, x.shape, 1)
    g = h // (N_HEADS // KV_HEADS)
    return jnp.where((lane >= HEAD_DIM * g) & (lane < HEAD_DIM * (g + 1)), x, 0.0)


def _sb_prompt_kernel(q_ref, kv_ref, uu_ref, o_ref, qs_ref, c_ref, acc_ref, *, tq):
    i = pl.program_id(0)
    rows = N_HEADS * tq
    for h in range(N_HEADS):
        qs_ref[h * tq:(h + 1) * tq, :] = q_ref[:, h * LANE:(h + 1) * LANE]
    c_ref[...] = jnp.zeros(c_ref.shape, F32)
    acc_ref[...] = jnp.zeros(acc_ref.shape, F32)

    def step(j, masked):
        kv = kv_ref[pl.ds(pl.multiple_of(j * tq, tq), tq), :]
        z = _dot_nt(qs_ref[...], kv[:, :LANE])
        ls, l1m = _log_sigmoid_pair(z)
        if masked:
            row = lax.broadcasted_iota(jnp.int32, (rows, tq), 0) & (tq - 1)
            col = lax.broadcasted_iota(jnp.int32, (rows, tq), 1)
            mask = col < row
            l1m = jnp.where(mask, l1m, 0.0)
        a = jnp.exp(ls + _suffix_sum(l1m, uu_ref[...]) + c_ref[...])
        if masked:
            a = jnp.where(mask, a, 0.0)
        acc_ref[...] += _dot(a.astype(BF16), kv[:, LANE:])
        c_ref[...] += jnp.sum(l1m, axis=-1, keepdims=True)

    step(i, True)

    def body(t, carry):
        step(i - 1 - 2 * t, False)
        step(i - 2 - 2 * t, False)
        return carry

    lax.fori_loop(0, i // 2, body, 0)

    @pl.when(i % 2 == 1)
    def _():
        step(0, False)

    acc = acc_ref[...]
    for h in range(N_HEADS):
        o_ref[:, h * LANE:(h + 1) * LANE] = _group_lanes(acc[h * tq:(h + 1) * tq], h)


def _suffix_matrix(tk):
    u = (np.arange(tk)[:, None] > np.arange(tk)[None, :]).astype(np.float32)
    return jnp.asarray(np.concatenate([u, u], axis=0), dtype=BF16)


def _sb_prompt(sq, skv, t_p, tq):
    return pl.pallas_call(
        functools.partial(_sb_prompt_kernel, tq=tq),
        grid=(t_p // tq,),
        in_specs=[pl.BlockSpec((tq, N_HEADS * LANE), lambda i: (i, 0)),
                  pl.BlockSpec((t_p, 2 * LANE), lambda i: (0, 0)),
                  _full((2 * tq, tq))],
        out_specs=pl.BlockSpec((tq, N_HEADS * LANE), lambda i: (i, 0)),
        out_shape=jax.ShapeDtypeStruct((t_p, N_HEADS * LANE), F32),
        scratch_shapes=[pltpu.VMEM((N_HEADS * tq, LANE), BF16), pltpu.VMEM((N_HEADS * tq, 1), F32),
                        pltpu.VMEM((N_HEADS * tq, LANE), F32)],
        compiler_params=_cparams("parallel"),
        name="sb_prompt",
    )(sq, skv, _suffix_matrix(tq))


def _top3_select(score, valid):
    lane = lax.broadcasted_iota(jnp.int32, score.shape, 1).astype(F32)
    s = jnp.where(valid, score, -jnp.inf)
    sel = jnp.zeros(score.shape, F32)
    for _ in range(MB_TOPK):
        mx = jnp.max(s, axis=-1, keepdims=True)
        cand = jnp.where(s == mx, lane, 1e9)
        cand = jnp.where(mx > -jnp.inf, cand, 1e9)
        pick = lane == jnp.min(cand, axis=-1, keepdims=True)
        sel = jnp.where(pick, 1.0, sel)
        s = jnp.where(pick, -jnp.inf, s)
    return sel


def _block_scores(qh, ql, kmean):
    kh, kl = _split_bf16(kmean)
    return _dot_nt(qh, kh) + _dot_nt(qh, kl) + _dot_nt(ql, kh)


def _moba_prompt_kernel(t5_ref, qh_ref, ql_ref, kx_ref, v_ref, kmean_ref, bias_ref, o_ref,
                        qx_ref, m_ref, acc_ref, *, tq):
    i = pl.program_id(0)
    _flash_init(m_ref, acc_ref)
    ones = jnp.ones((tq, LANE), BF16)
    lane = lax.broadcasted_iota(jnp.int32, (tq, LANE), 1)
    kmean = kmean_ref[...]
    for h in range(N_HEADS):
        qh = qh_ref[:, h * LANE:(h + 1) * LANE]
        sel = _top3_select(_block_scores(qh, ql_ref[:, h * LANE:(h + 1) * LANE], kmean), lane < i)
        far_hi, far_lo = _split_bf16(jnp.full((tq, LANE), t5_ref[T5_BUCKETS - 1, h], F32))
        hi = jnp.where(sel > 0.0, far_hi.astype(F32), NEG)
        hi = jnp.where(lane == i - 1, jnp.where(sel > 0.0, 0.0, NEG), hi)
        hi = jnp.where(lane == i, 0.0, hi)
        sel_up = pltpu.roll(sel, HALF_LANE, axis=1)
        lo = jnp.where((sel_up > 0.0) & (lane - HALF_LANE < i - 1), far_lo.astype(F32), 0.0)
        qbias = jnp.where(lane < HALF_LANE, hi, lo)
        qx_ref[h] = jnp.concatenate([qh, qbias.astype(BF16)], axis=-1)

    def step(j, near):
        rows = pl.ds(pl.multiple_of(j * tq, tq), tq)
        k = kx_ref[rows, :]
        v_ones = jnp.concatenate([v_ref[rows, :], ones], axis=-1)
        for h in range(N_HEADS):
            s = _dot_nt(qx_ref[h], k)
            if near == "own":
                row, col = _causal_ids(tq)
                s = jnp.where(col <= row, s + bias_ref[h * tq:(h + 1) * tq, tq:], NEG)
            elif near == "previous":
                s = s + bias_ref[h * tq:(h + 1) * tq, :tq]
            _flash_update(h, s, v_ones, m_ref, acc_ref)

    step(i, "own")

    @pl.when(i >= 1)
    def _():
        step(i - 1, "previous")

    _unrolled_blocks(jnp.maximum(i - 1, 0), lambda j: step(j, None))
    for h in range(N_HEADS):
        o_ref[:, h * LANE:(h + 1) * LANE] = _group_lanes(_flash_result(h, acc_ref), h)


def _moba_prompt(t5_bias, mqh, mql, mkx, mvb, kmean, bias, t_p):
    tq = MB_BLOCK
    assert t_p // MB_BLOCK <= HALF_LANE
    qspec = pl.BlockSpec((tq, N_HEADS * LANE), lambda i: (i, 0))
    return pl.pallas_call(
        functools.partial(_moba_prompt_kernel, tq=tq),
        grid=(t_p // tq,),
        in_specs=[pl.BlockSpec(memory_space=pltpu.SMEM), qspec, qspec,
                  pl.BlockSpec((t_p, 2 * LANE), lambda i: (0, 0)),
                  pl.BlockSpec((t_p, LANE), lambda i: (0, 0)),
                  _full(kmean.shape), _full(bias.shape)],
        out_specs=pl.BlockSpec((tq, N_HEADS * LANE), lambda i: (i, 0)),
        out_shape=jax.ShapeDtypeStruct((t_p, N_HEADS * LANE), F32),
        scratch_shapes=[pltpu.VMEM((N_HEADS, tq, 2 * LANE), BF16), pltpu.VMEM((N_HEADS, tq, LANE), F32),
                        pltpu.VMEM((N_HEADS, tq, 2 * LANE), F32)],
        compiler_params=_cparams("parallel"),
        name="moba_prompt",
    )(t5_bias, mqh, mql, mkx, mvb, kmean, bias)


def _kmean_kernel(k_ref, o_ref):
    k = k_ref[...]
    nb = k.shape[0] // MB_BLOCK
    o_ref[...] = jnp.sum(k.reshape(nb, MB_BLOCK, LANE), axis=1) * (1.0 / MB_BLOCK)


def _block_means(mk, t_p):
    nb = t_p // MB_BLOCK
    per = _pick_tile(nb, (8,))
    return pl.pallas_call(
        _kmean_kernel,
        grid=(nb // per,),
        in_specs=[pl.BlockSpec((per * MB_BLOCK, LANE), lambda i: (i, 0))],
        out_specs=pl.BlockSpec((per, LANE), lambda i: (i, 0)),
        out_shape=jax.ShapeDtypeStruct((nb, LANE), F32),
        compiler_params=_cparams("parallel"),
        name="moba_kmean",
    )(mk)


def _t5_kernel(t5_ref, bucket_ref, o_ref):
    bucket = bucket_ref[...]
    for h in range(N_HEADS):
        acc = jnp.zeros(bucket.shape, F32)
        for b in range(T5_BUCKETS):
            acc = jnp.where(bucket == b, t5_ref[b, h], acc)
        o_ref[h] = acc


def _t5_tiles(t5_bias, bucket):
    return pl.pallas_call(
        _t5_kernel,
        in_specs=[pl.BlockSpec(memory_space=pltpu.SMEM), pl.BlockSpec(memory_space=pltpu.VMEM)],
        out_specs=pl.BlockSpec(memory_space=pltpu.VMEM),
        out_shape=jax.ShapeDtypeStruct((N_HEADS,) + bucket.shape, F32),
        name="t5_tiles",
    )(t5_bias, bucket)


CHUNK_PAGES = 16
RING_SLOTS = 8
LOOKAHEAD = RING_SLOTS - 1
SAMPLE_ROWS = 4 * N_HEADS


class _PageStream:
    def __init__(self, pt_ref, layer, caches, rings, sem_ref, n_chunks):
        self.pt_ref, self.layer, self.caches, self.rings = pt_ref, layer, caches, rings
        self.sem_ref, self.n_chunks = sem_ref, n_chunks

    def _copies(self, g):
        slot = g % RING_SLOTS
        for r in range(CHUNK_PAGES):
            page = self.pt_ref[g * CHUNK_PAGES + r]
            for which, (cache, ring) in enumerate(zip(self.caches, self.rings)):
                yield pltpu.make_async_copy(cache.at[self.layer, page], ring.at[slot, r],
                                            self.sem_ref.at[slot, which])

    def start(self, g):
        for cp in self._copies(g):
            cp.start()

    def wait(self, g):
        for cp in self._copies(g):
            cp.wait()

    def consume(self, b, chunks_per_seq, use):
        @pl.when(b == 0)
        def _():
            for g in range(LOOKAHEAD):
                self.start(g)

        for k in range(chunks_per_seq):
            g = b * chunks_per_seq + k
            self.wait(g)

            @pl.when(g + LOOKAHEAD < self.n_chunks)
            def _():
                self.start(g + LOOKAHEAD)

            use(k, g % RING_SLOTS)


def _ring(cache):
    return pltpu.VMEM((RING_SLOTS, CHUNK_PAGES) + cache.shape[2:], cache.dtype)


def _feature_major(cache):
    l, pool, page = cache.shape[:3]
    perm = (0, 1) + tuple(range(3, cache.ndim)) + (2,)
    return cache.transpose(perm).reshape(l, pool, -1, page)


def _sample_tokens():
    return lax.broadcasted_iota(jnp.int32, (SAMPLE_ROWS, PAGE_SIZE), 0) // N_HEADS


def _own_group_lanes(x):
    rows = x.shape[0]
    lane = lax.broadcasted_iota(jnp.int32, (rows, LANE), 1)
    g = (lax.broadcasted_iota(jnp.int32, (rows, LANE), 0) % N_HEADS) // (N_HEADS // KV_HEADS)
    return jnp.where((lane >= HEAD_DIM * g) & (lane < HEAD_DIM * (g + 1)), x, 0.0)


def _mla_sample_kernel(pt_ref, q_ref, knew_ref, ckv_hbm, kpe_hbm, o_ref, s_ref, kbuf_ref,
                       ckv_ring, kpe_ring, sem_ref, *, n_pages, layer, n_chunks):
    q = q_ref[...]
    q_lat = q[:, :MLA_KV_LORA]
    q_pe = q[:, MLA_KV_LORA:MLA_KV_LORA + MLA_ROPE]

    def use(k, slot):
        for r in range(CHUNK_PAGES):
            pg = k * CHUNK_PAGES + r
            ckv = ckv_ring[slot, r].astype(BF16)
            kpe_t = kpe_ring[slot, r].astype(BF16)
            kbuf_ref[pg] = ckv
            s_ref[pg] = (_dot_nt(q_lat, ckv) + _dot(q_pe, kpe_t)) * MLA_SCALE

    stream = _PageStream(pt_ref, layer, (ckv_hbm, kpe_hbm), (ckv_ring, kpe_ring), sem_ref, n_chunks)
    stream.consume(pl.program_id(0), n_pages // CHUNK_PAGES, use)

    kn = knew_ref[...]
    s_new = _dot_nt(q, kn) * MLA_SCALE
    col = lax.broadcasted_iota(jnp.int32, s_new.shape, 1)
    s_new = jnp.where(col <= _sample_tokens(), s_new, NEG)
    s_all = s_ref[...]
    m = jnp.maximum(jnp.max(jnp.max(s_all, axis=0), axis=-1, keepdims=True),
                    jnp.max(s_new, axis=-1, keepdims=True))
    p_new = jnp.exp(s_new - m)
    p_all = jnp.exp(s_all - m)
    l = (jnp.sum(p_new, axis=-1, keepdims=True)
         + jnp.sum(jnp.sum(p_all, axis=0), axis=-1, keepdims=True))
    acc = _dot(p_new.astype(BF16), kn[:, :MLA_KV_LORA])
    for pg in range(n_pages):
        acc = acc + _dot(p_all[pg].astype(BF16), kbuf_ref[pg])
    o_ref[...] = acc / l


def _sample_call(kernel, name, pt, layer, inputs, in_specs, caches, out_width, scratch_shapes):
    b = inputs[0].shape[0] // SAMPLE_ROWS
    n_pages = pt.shape[0] // b
    assert n_pages % CHUNK_PAGES == 0
    n_chunks = b * n_pages // CHUNK_PAGES
    assert n_chunks >= LOOKAHEAD
    grid_spec = pltpu.PrefetchScalarGridSpec(
        num_scalar_prefetch=1, grid=(b,),
        in_specs=in_specs + [pl.BlockSpec(memory_space=pl.ANY)] * len(caches),
        out_specs=_row_spec(out_width),
        scratch_shapes=scratch_shapes + [_ring(c) for c in caches]
        + [pltpu.SemaphoreType.DMA((RING_SLOTS, len(caches)))])
    return pl.pallas_call(
        functools.partial(kernel, n_pages=n_pages, layer=layer, n_chunks=n_chunks),
        grid_spec=grid_spec,
        out_shape=jax.ShapeDtypeStruct((b * SAMPLE_ROWS, out_width), F32),
        compiler_params=_cparams("arbitrary"),
        name=name,
    )(pt, *inputs, *caches)


def _row_spec(width):
    return pl.BlockSpec((SAMPLE_ROWS, width), lambda i, pt: (i, 0))


def _new_spec(width):
    return pl.BlockSpec((None, PAGE_SIZE, width), lambda i, pt: (i, 0, 0))


def _mla_sample(pt, q, knew, cache_ckv, cache_kpe, layer):
    n_pages = pt.shape[0] // (q.shape[0] // SAMPLE_ROWS)
    return _sample_call(
        _mla_sample_kernel, "mla_sample", pt, layer, [q, knew],
        [_row_spec(2 * LANE), _new_spec(2 * LANE)], [cache_ckv, cache_kpe], MLA_KV_LORA,
        [pltpu.VMEM((n_pages, SAMPLE_ROWS, PAGE_SIZE), F32),
         pltpu.VMEM((n_pages, PAGE_SIZE, MLA_KV_LORA), BF16)])


def _sb_sample_kernel(pt_ref, q_ref, knew_ref, vnew_ref, uu_ref, k_hbm, v_hbm, o_ref, z_ref, vbuf_ref,
                      k_ring, v_ring, sem_ref, *, n_pages, layer, n_chunks):
    q = q_ref[...]

    def use(k, slot):
        for r in range(CHUNK_PAGES):
            pg = k * CHUNK_PAGES + r
            z_ref[pg] = _dot(q, k_ring[slot, r].astype(BF16))
            vbuf_ref[pg] = v_ring[slot, r].astype(BF16)

    stream = _PageStream(pt_ref, layer, (k_hbm, v_hbm), (k_ring, v_ring), sem_ref, n_chunks)
    stream.consume(pl.program_id(0), n_pages // CHUNK_PAGES, use)

    uu = uu_ref[...]
    ls, l1m = _log_sigmoid_pair(_dot_nt(q, knew_ref[...]))
    col = lax.broadcasted_iota(jnp.int32, ls.shape, 1)
    mask = col < _sample_tokens()
    l1m = jnp.where(mask, l1m, 0.0)
    within = _suffix_sum(l1m, uu)
    a = jnp.where(mask, jnp.exp(ls + within), 0.0)
    acc = _dot(a.astype(BF16), vnew_ref[...])
    carry = within[:, 0:1] + l1m[:, 0:1]
    ls, l1m = _log_sigmoid_pair(z_ref[...])
    within = _suffix_sum(l1m.reshape(n_pages * SAMPLE_ROWS, PAGE_SIZE), uu)
    within = within.reshape(n_pages, SAMPLE_ROWS, PAGE_SIZE)
    for pg in reversed(range(n_pages)):
        a = jnp.exp(ls[pg] + within[pg] + carry)
        acc = acc + _dot_nt(a.astype(BF16), vbuf_ref[pg])
        carry = carry + within[pg][:, 0:1] + l1m[pg][:, 0:1]
    o_ref[...] = _own_group_lanes(acc)


def _sb_sample(pt, q, knew, vnew, cache_k, cache_v, layer):
    n_pages = pt.shape[0] // (q.shape[0] // SAMPLE_ROWS)
    return _sample_call(
        _sb_sample_kernel, "sb_sample", pt, layer, [q, knew, vnew, _suffix_matrix(PAGE_SIZE)],
        [_row_spec(LANE), _new_spec(LANE), _new_spec(LANE),
         pl.BlockSpec((2 * PAGE_SIZE, PAGE_SIZE), lambda i, pt: (0, 0))],
        [cache_k, cache_v], LANE,
        [pltpu.VMEM((n_pages, SAMPLE_ROWS, PAGE_SIZE), F32),
         pltpu.VMEM((n_pages, LANE, PAGE_SIZE), BF16)])


def _moba_sample_kernel(pt_ref, qh_ref, ql_ref, knew_ref, vnew_ref, bias_ref, far_ref, k_hbm, v_hbm,
                        o_ref, s_ref, vbuf_ref, kmean_ref, k_ring, v_ring, sem_ref,
                        *, n_pages, layer, n_chunks):
    qh = qh_ref[...]
    pages_per_block = MB_BLOCK // PAGE_SIZE
    n_blocks = n_pages // pages_per_block
    kmean_ref[...] = jnp.zeros(kmean_ref.shape, F32)
    blk_lane = lax.broadcasted_iota(jnp.int32, kmean_ref.shape, 1)

    def use(k, slot):
        for r in range(0, CHUNK_PAGES, pages_per_block):
            ksum = jnp.zeros((LANE, 1), F32)
            for rr in range(r, r + pages_per_block):
                pg = k * CHUNK_PAGES + rr
                page = k_ring[slot, rr]
                s_ref[pg] = _dot(qh, page.astype(BF16))
                vbuf_ref[pg] = v_ring[slot, rr].astype(BF16)
                ksum = ksum + jnp.sum(page, axis=1, keepdims=True)
            blk = (k * CHUNK_PAGES + r) // pages_per_block
            kmean_ref[...] += jnp.where(blk_lane == blk, ksum * (1.0 / MB_BLOCK), 0.0)

    stream = _PageStream(pt_ref, layer, (k_hbm, v_hbm), (k_ring, v_ring), sem_ref, n_chunks)
    stream.consume(pl.program_id(0), n_pages // CHUNK_PAGES, use)

    kh, kl = _split_bf16(kmean_ref[...])
    score = _dot(qh, kh) + _dot(qh, kl) + _dot(ql_ref[...], kh)
    blk = lax.broadcasted_iota(jnp.int32, score.shape, 1)
    sel = _top3_select(score, blk < n_blocks)
    far_bias = far_ref[...]
    bias = bias_ref[...]
    s_new = _dot_nt(qh, knew_ref[...]) + bias[:, MB_BLOCK:MB_BLOCK + PAGE_SIZE]
    col = lax.broadcasted_iota(jnp.int32, s_new.shape, 1)
    s_new = jnp.where(col <= _sample_tokens(), s_new, NEG)
    m_lanes = s_new
    masked = []
    for n in range(n_blocks):
        keep = sel[:, n:n + 1] > 0.0
        for pg in range(n * pages_per_block, (n + 1) * pages_per_block):
            if n == n_blocks - 1:
                off = (pg - n * pages_per_block) * PAGE_SIZE
                s = s_ref[pg] + bias[:, off:off + PAGE_SIZE]
            else:
                s = s_ref[pg] + far_bias
            s = jnp.where(keep, s, NEG)
            masked.append(s)
            m_lanes = jnp.maximum(m_lanes, s)
    m = jnp.max(m_lanes, axis=-1, keepdims=True)
    p_new = jnp.exp(s_new - m)
    l_lanes = p_new
    acc = _dot(p_new.astype(BF16), vnew_ref[...])
    for pg in range(n_pages):
        p = jnp.exp(masked[pg] - m)
        l_lanes = l_lanes + p
        acc = acc + _dot_nt(p.astype(BF16), vbuf_ref[pg])
    o_ref[...] = _own_group_lanes(acc / jnp.sum(l_lanes, axis=-1, keepdims=True))


def _moba_sample(pt, qh, ql, knew, vnew, bias, far_bias, cache_k, cache_v, layer, nbp):
    n_pages = pt.shape[0] // (qh.shape[0] // SAMPLE_ROWS)
    return _sample_call(
        _moba_sample_kernel, "moba_sample", pt, layer, [qh, ql, knew, vnew, bias, far_bias],
        [_row_spec(LANE), _row_spec(LANE), _new_spec(LANE), _new_spec(LANE),
         pl.BlockSpec(bias.shape, lambda i, pt: (0, 0)), pl.BlockSpec(far_bias.shape, lambda i, pt: (0, 0))],
        [cache_k, cache_v], LANE,
        [pltpu.VMEM((n_pages, SAMPLE_ROWS, PAGE_SIZE), F32),
         pltpu.VMEM((n_pages, LANE, PAGE_SIZE), BF16),
         pltpu.VMEM((LANE, nbp), F32)])


CONV_HALO = 32


def _conv_prompt_kernel(cur_ref, prev_ref, w_ref, b_ref, g_ref, beta_ref, o_ref, ext_ref, *, tm):
    i = pl.program_id(0)
    prev = prev_ref[...]
    ext_ref[:CONV_HALO, :] = jnp.where(i > 0, prev, 0.0)
    ext_ref[CONV_HALO:, :] = cur_ref[...]
    acc = jnp.zeros((tm, CONV_CH), F32)
    base = CONV_HALO - (CONV_WIDTH - 1)
    for w in range(CONV_WIDTH):
        acc = acc + ext_ref[base + w:base + w + tm, :] * w_ref[w:w + 1, :]
    o_ref[...] = _silu(_layer_norm(acc + b_ref[...], g_ref[...], beta_ref[...]))


def _conv_prompt(glu, t_p, conv_w, conv_b, ln_g, ln_b):
    tm = _pick_tile(t_p, (512, 256, 128, 64, 32))
    per = tm // CONV_HALO
    return pl.pallas_call(
        functools.partial(_conv_prompt_kernel, tm=tm),
        grid=(t_p // tm,),
        in_specs=[pl.BlockSpec((tm, CONV_CH), lambda i: (i, 0)),
                  pl.BlockSpec((CONV_HALO, CONV_CH), lambda i: (jnp.maximum(i * per - 1, 0), 0)),
                  _full(conv_w.shape), _full(conv_b.shape), _full(ln_g.shape), _full(ln_b.shape)],
        out_specs=pl.BlockSpec((tm, CONV_CH), lambda i: (i, 0)),
        out_shape=jax.ShapeDtypeStruct((t_p, CONV_CH), F32),
        scratch_shapes=[pltpu.VMEM((tm + CONV_HALO, CONV_CH), F32)],
        compiler_params=_cparams("parallel"),
        name="conv_prompt",
    )(glu, glu, conv_w, conv_b, ln_g, ln_b)


def _conv_sample_kernel(ext_ref, w_ref, b_ref, g_ref, beta_ref, o_ref):
    n_new = o_ref.shape[0]
    for t in range(n_new):
        acc = jnp.zeros(o_ref.shape[1:], F32)
        for w in range(CONV_WIDTH):
            acc = acc + ext_ref[t + w] * w_ref[w:w + 1, :]
        o_ref[t] = _silu(_layer_norm(acc + b_ref[...], g_ref[...], beta_ref[...]))


def _conv_sample(ext_t, conv_w, conv_b, ln_g, ln_b):
    n_new = ext_t.shape[0] - (CONV_WIDTH - 1)
    vm = pl.BlockSpec(memory_space=pltpu.VMEM)
    return pl.pallas_call(
        _conv_sample_kernel,
        in_specs=[vm] * 5,
        out_specs=vm,
        out_shape=jax.ShapeDtypeStruct((n_new,) + ext_t.shape[1:], F32),
        name="conv_sample",
    )(ext_t, conv_w, conv_b, ln_g, ln_b)


def _merge_kernel(x_ref, cact_ref, lat_ref, sb_ref, mb_ref, gate_ref, gb_ref, wc_ref, wuv_ref, wm_ref,
                  ws_ref, wo_ref, wout_ref, g_ref, b_ref, o_ref, *, alpha):
    c = _dot(cact_ref[...].astype(BF16), wc_ref[...])
    mv = _dot(lat_ref[...].astype(BF16), wuv_ref[...])
    m = _dot(mv.astype(BF16), wm_ref[...])
    s = _dot(sb_ref[...].astype(BF16), ws_ref[...])
    o = _dot(mb_ref[...].astype(BF16), wo_ref[...])
    g = jax.nn.sigmoid(gate_ref[...] + gb_ref[...])
    d = D_MODEL
    y = g[:, :d] * c + g[:, d:2 * d] * m + g[:, 2 * d:3 * d] * s + g[:, 3 * d:] * o
    y = _dot(y.astype(BF16), wout_ref[...])
    o_ref[...] = _layer_norm(alpha * x_ref[...] + y, g_ref[...], b_ref[...])


def _merge(x, cact, lat, sb, mb, gate, gate_b, wc, wuv, wm, ws, wo, wout, ln_g, ln_b, alpha):
    n = x.shape[0]
    tm = _pick_tile(n, (256, 128, 64, 32, 16, 8))
    row = lambda w: pl.BlockSpec((tm, w), lambda i: (i, 0))
    weights = (gate_b, wc, wuv, wm, ws, wo, wout, ln_g, ln_b)
    return pl.pallas_call(
        functools.partial(_merge_kernel, alpha=alpha),
        grid=(n // tm,),
        in_specs=[row(a.shape[1]) for a in (x, cact, lat, sb, mb, gate)] + [_full(w.shape) for w in weights],
        out_specs=row(D_MODEL),
        out_shape=jax.ShapeDtypeStruct((n, D_MODEL), F32),
        compiler_params=_cparams("parallel"),
        name="merge",
    )(x, cact, lat, sb, mb, gate, *weights)


EXPERTS_PER_STEP = 4


def _first_index_of_max(s, idx):
    mx = jnp.max(s, axis=0, keepdims=True)
    first = jnp.min(jnp.where(s == mx, idx, 1e9), axis=0, keepdims=True)
    return idx == first


def _pick_column(x, j):
    lane = lax.broadcasted_iota(jnp.int32, x.shape, 1)
    return jnp.sum(jnp.where(lane == j, x, 0.0), axis=-1, keepdims=True)


def _router_kernel(x_ref, wh_ref, wl_ref, bias_ref, gate_ref, xb_ref):
    x = x_ref[...]
    xh, xl = _split_bf16(x)
    xb_ref[...] = xh
    logits = (_dot_nt(wh_ref[...], xh) + _dot_nt(wl_ref[...], xh) + _dot_nt(wh_ref[...], xl))[:N_EXPERTS]
    scores = jax.nn.sigmoid(logits)
    biased = scores + bias_ref[...]
    e, tm = biased.shape
    per = e // N_GROUPS
    grp = biased.reshape(N_GROUPS, per, tm)
    sub = lax.broadcasted_iota(jnp.int32, grp.shape, 1).astype(F32)
    top1 = jnp.max(grp, axis=1, keepdims=True)
    first = jnp.min(jnp.where(grp == top1, sub, 1e9), axis=1, keepdims=True)
    top2 = jnp.max(jnp.where(sub == first, -jnp.inf, grp), axis=1, keepdims=True)
    gs = jnp.broadcast_to(top1 + top2, grp.shape).reshape(e, tm)
    eidx = lax.broadcasted_iota(jnp.int32, biased.shape, 0)
    gidx = (eidx // per).astype(F32)
    e_keep = jnp.zeros(biased.shape, F32)
    for _ in range(TOPK_GROUPS):
        pick = _first_index_of_max(gs, gidx)
        e_keep = jnp.where(pick, 1.0, e_keep)
        gs = jnp.where(pick, -jnp.inf, gs)
    eidx = eidx.astype(F32)
    cand = jnp.where(e_keep > 0.0, biased, -jnp.inf)
    chosen = jnp.zeros(biased.shape, F32)
    for _ in range(TOP_K):
        pick = _first_index_of_max(cand, eidx)
        chosen = jnp.where(pick, 1.0, chosen)
        cand = jnp.where(pick, -jnp.inf, cand)
    sel = chosen * scores
    wts = sel / jnp.sum(sel, axis=0, keepdims=True) * ROUTED_SCALE
    gate_ref[...] = jnp.concatenate([wts, jnp.zeros_like(wts)], axis=0).T


def _router(x, router_w, router_bias):
    n = x.shape[0]
    tm = _pick_tile(n, (512, 256, 128))
    wt = jnp.pad(router_w.T, ((0, LANE - N_EXPERTS), (0, 0)))
    wh, wl = _split_bf16(wt)
    return pl.pallas_call(
        _router_kernel,
        grid=(n // tm,),
        in_specs=[pl.BlockSpec((tm, D_MODEL), lambda i: (i, 0)), _full(wh.shape), _full(wl.shape),
                  _full((N_EXPERTS, 1))],
        out_specs=[pl.BlockSpec((tm, LANE), lambda i: (i, 0)),
                   pl.BlockSpec((tm, D_MODEL), lambda i: (i, 0))],
        out_shape=[jax.ShapeDtypeStruct((n, LANE), F32), jax.ShapeDtypeStruct((n, D_MODEL), BF16)],
        compiler_params=_cparams("parallel"),
        name="moe_router",
    )(x, wh, wl, router_bias.reshape(N_EXPERTS, 1))


def _experts_kernel(x_ref, xb_ref, gate_ref, wg_ref, wu_ref, wd_ref, wsg_ref, wsu_ref, wsd_ref,
                    g_ref, b_ref, o_ref, acc_ref, *, alpha):
    step = pl.program_id(1)
    xb = xb_ref[...]

    @pl.when(step == 0)
    def _():
        h = _silu(_dot(xb, wsg_ref[...])) * _dot(xb, wsu_ref[...])
        acc_ref[...] = _dot(h.astype(BF16), wsd_ref[...])

    gate = gate_ref[...]
    hs = []
    for k in range(EXPERTS_PER_STEP):
        w = _pick_column(gate, step * EXPERTS_PER_STEP + k)
        h = _silu(_dot(xb, wg_ref[k])) * _dot(xb, wu_ref[k]) * w
        hs.append(h.astype(BF16))
    acc_ref[...] += _dot(jnp.concatenate(hs, axis=-1), wd_ref[...])

    @pl.when(step == pl.num_programs(1) - 1)
    def _():
        o_ref[...] = _layer_norm(alpha * x_ref[...] + acc_ref[...], g_ref[...], b_ref[...])


def _experts(x, xb, gate, wg, wu, wd, wsg, wsu, wsd, ln_g, ln_b, alpha):
    n = x.shape[0]
    tm = _pick_tile(n, (768, 512, 256, 128))
    n_exp, ff = wg.shape[0], wg.shape[2]
    wd = wd.reshape(n_exp // EXPERTS_PER_STEP, EXPERTS_PER_STEP * ff, D_MODEL)
    row = lambda w: pl.BlockSpec((tm, w), lambda i, e: (i, 0))
    return pl.pallas_call(
        functools.partial(_experts_kernel, alpha=alpha),
        grid=(n // tm, n_exp // EXPERTS_PER_STEP),
        in_specs=[row(D_MODEL), row(D_MODEL), row(LANE),
                  pl.BlockSpec((EXPERTS_PER_STEP, D_MODEL, ff), lambda i, e: (e, 0, 0)),
                  pl.BlockSpec((EXPERTS_PER_STEP, D_MODEL, ff), lambda i, e: (e, 0, 0)),
                  pl.BlockSpec((None, EXPERTS_PER_STEP * ff, D_MODEL), lambda i, e: (e, 0, 0)),
                  _full(wsg.shape), _full(wsu.shape), _full(wsd.shape), _full(ln_g.shape), _full(ln_b.shape)],
        out_specs=row(D_MODEL),
        out_shape=jax.ShapeDtypeStruct((n, D_MODEL), F32),
        scratch_shapes=[pltpu.VMEM((tm, D_MODEL), F32)],
        compiler_params=_cparams("parallel", "arbitrary"),
        name="moe_experts",
    )(x, xb, gate, wg, wu, wd, wsg, wsu, wsd, ln_g, ln_b)


def _vec(v):
    return v.reshape(1, -1)


def _pad_new(a):
    return jnp.pad(a, ((0, 0), (0, PAGE_SIZE - a.shape[1]), (0, 0))).astype(BF16)


def kernel(x_prompt, x_sample, cache_mla_ckv, cache_mla_kpe, cache_sb_k, cache_sb_v, cache_moba_k, cache_moba_v, state_conv, page_table, t5_bias, w_in, gate_b, conv_w, conv_b, conv_ln_g, conv_ln_b, w_conv_out, mla_q_norm, w_q_up, mla_kv_norm, w_kv_up, w_mla_out, w_sb_out, w_mb_out, w_out, ln1_g, ln1_b, router_w, router_bias, w_e_gate, w_e_up, w_e_down, w_s_gate, w_s_up, w_s_down, ln2_g, ln2_b):
    depth = w_in.shape[0]
    alpha = (2 * depth) ** 0.25
    bp, t_p, _ = x_prompt.shape
    assert bp == 1
    bs, n_new, _ = x_sample.shape
    n_s = bs * n_new
    n_pages = page_table.shape[1]
    past = n_pages * PAGE_SIZE
    assert n_new * N_HEADS == SAMPLE_ROWS
    assert t_p % MB_BLOCK == 0 and past % MB_BLOCK == 0

    x = jnp.concatenate([x_prompt.reshape(t_p, D_MODEL), x_sample.reshape(n_s, D_MODEL)], axis=0)
    pos = jnp.concatenate([jnp.arange(t_p, dtype=jnp.int32),
                           jnp.tile(past + jnp.arange(n_new, dtype=jnp.int32), bs)])
    cos, sin = _rope_tables(pos)
    pt = page_table.reshape(-1)
    kpe_t, sbk_t, sbv_t, mbk_t, mbv_t = (
        _feature_major(c) for c in (cache_mla_kpe, cache_sb_k, cache_sb_v, cache_moba_k, cache_moba_v))

    tq = MB_BLOCK
    rel_p = jnp.arange(tq)[:, None] - jnp.arange(2 * tq)[None, :] + tq
    bias_p = _t5_tiles(t5_bias, _t5_bucket(rel_p).astype(jnp.int32)).reshape(N_HEADS * tq, 2 * tq)
    kpos_s = jnp.concatenate([past - MB_BLOCK + jnp.arange(MB_BLOCK), past + jnp.arange(PAGE_SIZE)])
    rel_s = past + jnp.arange(8)[:, None] - kpos_s[None, :]
    bias_s = _t5_tiles(t5_bias, _t5_bucket(rel_s).astype(jnp.int32))
    bias_s = bias_s[:, :n_new].transpose(1, 0, 2).reshape(SAMPLE_ROWS, MB_BLOCK + PAGE_SIZE)
    far_s = jnp.tile(t5_bias[T5_BUCKETS - 1], n_new).reshape(SAMPLE_ROWS, 1)

    nb_p = t_p // MB_BLOCK
    nb_s = past // MB_BLOCK
    nbp_p = -(-nb_p // LANE) * LANE
    nbp_s = -(-nb_s // LANE) * LANE

    new_p = {k: [] for k in ("ckv", "kpe", "sk", "sv", "mk", "mv", "conv")}
    new_s = {k: [] for k in new_p}
    for l in range(depth):
        w1, wg, wq, wuk, wuv = _pack_layer_weights(w_in[l], w_q_up[l], w_kv_up[l])
        pc = _project(x, cos, sin, w1, wg, wq, wuk, _vec(mla_q_norm[l]), _vec(mla_kv_norm[l]), t_p)
        for k in ("ckv", "kpe", "sk", "sv", "mk", "mv"):
            new_p[k].append(pc[k][:t_p])
            new_s[k].append(pc[k][t_p:])

        lat_p = _mla_prompt(pc["qcat"], pc["kcat"], t_p, _pick_tile(t_p, (256, 128)))
        sb_p = _sb_prompt(pc["sq"], pc["skv"], t_p, _pick_tile(t_p, (256, 128)))
        kmean = jnp.pad(_block_means(pc["mk"], t_p), ((0, nbp_p - nb_p), (0, 0)))
        mb_p = _moba_prompt(t5_bias, pc["mqh"], pc["mql"], pc["mkx"], pc["mvb"], kmean, bias_p, t_p)
        glu_p = pc["glu"][:t_p]
        cact_p = _conv_prompt(pc["glu"], t_p, conv_w[l], _vec(conv_b[l]), _vec(conv_ln_g[l]), _vec(conv_ln_b[l]))
        new_p["conv"].append(glu_p[t_p - (CONV_WIDTH - 1):])

        def rows(a, width):
            return a[t_p:].reshape(n_s * N_HEADS, width)

        def newkeys(a):
            return _pad_new(a[t_p:].reshape(bs, n_new, a.shape[1]))

        lat_s = _mla_sample(pt, rows(pc["qcat"], 2 * LANE), newkeys(pc["kcat"]),
                            cache_mla_ckv, kpe_t, l)
        skv_new = newkeys(pc["skv"])
        sb_s = _sb_sample(pt, rows(pc["sq"], LANE), skv_new[..., :LANE], skv_new[..., LANE:], sbk_t, sbv_t, l)
        mb_s = _moba_sample(pt, rows(pc["mqh"], LANE), rows(pc["mql"], LANE),
                            newkeys(pc["mkx"])[..., :LANE], newkeys(pc["mvb"]), bias_s, far_s,
                            mbk_t, mbv_t, l, nbp_s)
        ext_t = jnp.concatenate([state_conv[l].transpose(1, 0, 2),
                                 pc["glu"][t_p:].reshape(bs, n_new, CONV_CH).transpose(1, 0, 2)], axis=0)
        new_s["conv"].append(ext_t[n_new:].transpose(1, 0, 2))
        cact_s = _conv_sample(ext_t, conv_w[l], _vec(conv_b[l]), _vec(conv_ln_g[l]), _vec(conv_ln_b[l]))
        cact_s = cact_s.transpose(1, 0, 2).reshape(n_s, CONV_CH)

        cact = jnp.concatenate([cact_p, cact_s], axis=0)
        lat = jnp.concatenate([lat_p, lat_s.reshape(n_s, MLA_HEADS * MLA_KV_LORA)], axis=0)
        sb = jnp.concatenate([sb_p, sb_s.reshape(n_s, N_HEADS * LANE)], axis=0)
        mb = jnp.concatenate([mb_p, mb_s.reshape(n_s, N_HEADS * LANE)], axis=0)
        x1 = _merge(x, cact, lat, sb, mb, pc["gate"], _vec(gate_b[l]),
                    w_conv_out[l].astype(BF16), wuv, w_mla_out[l].astype(BF16),
                    _head_pad_cols(w_sb_out[l].T).T.astype(BF16), _head_pad_cols(w_mb_out[l].T).T.astype(BF16),
                    w_out[l].astype(BF16), _vec(ln1_g[l]), _vec(ln1_b[l]), alpha)
        gate, xb = _router(x1, router_w[l], router_bias[l])
        x = _experts(x1, xb, gate, w_e_gate[l].astype(BF16), w_e_up[l].astype(BF16), w_e_down[l].astype(BF16),
                     w_s_gate[l].astype(BF16), w_s_up[l].astype(BF16), w_s_down[l].astype(BF16),
                     _vec(ln2_g[l]), _vec(ln2_b[l]), alpha)

    def stack_p(k, tail):
        return jnp.stack(new_p[k]).reshape((depth, 1, t_p) + tail)

    def stack_s(k, tail):
        return jnp.stack(new_s[k]).reshape((depth, bs, n_new) + tail)

    kv = (KV_HEADS, HEAD_DIM)
    return (x[:t_p].reshape(1, t_p, D_MODEL), x[t_p:].reshape(bs, n_new, D_MODEL),
            stack_p("ckv", (MLA_KV_LORA,)), stack_s("ckv", (MLA_KV_LORA,)),
            stack_p("kpe", (MLA_ROPE,)), stack_s("kpe", (MLA_ROPE,)),
            stack_p("sk", kv), stack_s("sk", kv), stack_p("sv", kv), stack_s("sv", kv),
            stack_p("mk", kv), stack_s("mk", kv), stack_p("mv", kv), stack_s("mv", kv),
            jnp.stack(new_p["conv"]).reshape(depth, 1, CONV_WIDTH - 1, CONV_CH),
            jnp.stack(new_s["conv"]))
```

```python
import functools
import math

import numpy as np
import jax
import jax.numpy as jnp
from jax import lax
from jax.experimental import pallas as pl
from jax.experimental.pallas import tpu as pltpu

F32 = jnp.float32
BF16 = jnp.bfloat16

D_MODEL = 1024
HEAD_DIM = 64
CONV_CH = 256
CONV_WIDTH = 31
MLA_HEADS = 4
MLA_Q_LORA = 256
MLA_KV_LORA = 128
MLA_NOPE = 64
MLA_ROPE = 32
MLA_V = 64
MLA_SCALE = (MLA_NOPE + MLA_ROPE) ** -0.5
ROPE_THETA = 10000.0
N_HEADS = 4
KV_HEADS = 2
QK_SCALE = HEAD_DIM ** -0.5
MB_BLOCK = 256
MB_TOPK = 3
T5_BUCKETS = 32
T5_MAX_DIST = 128
N_BRANCH = 4
N_EXPERTS = 64
EXPERT_FF = 256
SHARED_FF = 256
TOP_K = 8
N_GROUPS = 8
TOPK_GROUPS = 4
ROUTED_SCALE = 2.5
LN_EPS = 1e-5
PAGE_SIZE = 128

LANE = 128
HALF_LANE = LANE // 2
NEG = -1e30
LOG2_E = math.log2(math.e)
VMEM_LIMIT = 56 * 1024 * 1024

NT_DIMS = (((1,), (1,)), ((), ()))


def _cparams(*sem):
    return pltpu.CompilerParams(dimension_semantics=sem, vmem_limit_bytes=VMEM_LIMIT)


def _pick_tile(n, cands):
    for c in cands:
        if n % c == 0:
            return c
    raise ValueError(f"no tile for {n}")


def _full(shape):
    nd = len(shape)
    return pl.BlockSpec(shape, lambda *_: (0,) * nd)


def _dot(a, b):
    return jnp.dot(a, b, preferred_element_type=F32)


def _dot_nt(a, b):
    return lax.dot_general(a, b, NT_DIMS, preferred_element_type=F32)


def _split_bf16(x):
    hi = x.astype(BF16)
    lo = (x - hi.astype(F32)).astype(BF16)
    return hi, lo


def _layer_norm(x, g, b):
    mu = jnp.mean(x, axis=-1, keepdims=True)
    d = x - mu
    var = jnp.mean(d * d, axis=-1, keepdims=True)
    return d * lax.rsqrt(var + LN_EPS) * g + b


def _rms_norm(x, g):
    return x * lax.rsqrt(jnp.mean(x * x, axis=-1, keepdims=True) + LN_EPS) * g


def _silu(x):
    return x * jax.nn.sigmoid(x)


def _head_pad_cols(w):
    k = w.shape[0]
    wh = w.reshape(k, N_HEADS, HEAD_DIM)
    z = jnp.zeros_like(wh)
    per_head = [jnp.concatenate([wh[:, h], z[:, h]] if h // 2 == 0 else [z[:, h], wh[:, h]], axis=-1)
                for h in range(N_HEADS)]
    return jnp.concatenate(per_head, axis=-1)


def _rot_cols(w):
    half = w.shape[-1] // 2
    return jnp.concatenate([-w[..., half:], w[..., :half]], axis=-1)


def _pad_cols(w, width):
    return jnp.pad(w, ((0, 0), (0, width - w.shape[-1])))


C_CONV, C_CQ, C_CKV, C_SQ, C_SK, C_SV, C_MQ, C_MK, C_MV, C_KPE, C_KROT, C_END = (
    0, 512, 768, 896, 1408, 1536, 1664, 2176, 2304, 2432, 2560, 2688)


def _pack_layer_weights(w_in, w_q_up, w_kv_up):
    pts = np.cumsum((2 * CONV_CH, MLA_Q_LORA, MLA_KV_LORA, MLA_ROPE, 256, 128, 128, 256, 128, 128))
    cv, cq, ckv, kpe, sq, sk, sv, mq, mk, mv, gate = jnp.split(w_in, pts.tolist(), axis=-1)
    w1 = jnp.concatenate([
        cv, cq, ckv, _head_pad_cols(sq), sk, sv, _head_pad_cols(mq), mk, mv,
        _pad_cols(kpe, LANE), _pad_cols(_rot_cols(kpe), LANE)], axis=-1).astype(BF16)
    wq = w_q_up.reshape(MLA_Q_LORA, MLA_HEADS, MLA_NOPE + MLA_ROPE)
    nope = wq[:, :, :MLA_NOPE].reshape(MLA_Q_LORA, MLA_HEADS * MLA_NOPE)
    pe = [_pad_cols(wq[:, h, MLA_NOPE:], LANE) for h in range(MLA_HEADS)]
    rot = [_pad_cols(_rot_cols(wq[:, h, MLA_NOPE:]), LANE) for h in range(MLA_HEADS)]
    wq_p = jnp.concatenate([nope] + pe + rot, axis=-1).astype(BF16)
    wkv = w_kv_up.reshape(MLA_KV_LORA, MLA_HEADS, MLA_NOPE + MLA_V)
    wuk = jnp.zeros((MLA_HEADS * MLA_NOPE, MLA_HEADS * MLA_KV_LORA), F32)
    wuv = jnp.zeros((MLA_HEADS * MLA_KV_LORA, MLA_HEADS * MLA_V), F32)
    for h in range(MLA_HEADS):
        wuk = wuk.at[h * MLA_NOPE:(h + 1) * MLA_NOPE, h * MLA_KV_LORA:(h + 1) * MLA_KV_LORA].set(
            wkv[:, h, :MLA_NOPE].T)
        wuv = wuv.at[h * MLA_KV_LORA:(h + 1) * MLA_KV_LORA, h * MLA_V:(h + 1) * MLA_V].set(
            wkv[:, h, MLA_NOPE:])
    return w1, gate.astype(BF16), wq_p, wuk.astype(BF16), wuv.astype(BF16)


def _rope_tables(pos):
    half = MLA_ROPE // 2
    inv = ROPE_THETA ** (-jnp.arange(half, dtype=F32) / half)
    ang = pos.astype(F32)[:, None] * inv[None, :]
    cos = jnp.tile(jnp.cos(ang), (1, 2 * MLA_HEADS))
    sin = jnp.tile(jnp.sin(ang), (1, 2 * MLA_HEADS))
    return cos, sin


def _t5_bucket(rel):
    n = jnp.maximum(rel, 0)
    exact = T5_BUCKETS // 2
    nf = jnp.maximum(n, 1).astype(F32)
    large = exact + (jnp.log(nf / exact) / math.log(T5_MAX_DIST / exact)
                     * (T5_BUCKETS - exact)).astype(jnp.int32)
    large = jnp.minimum(large, T5_BUCKETS - 1)
    return jnp.where(n < exact, n, large)


def _proj_kernel(x_ref, cos_ref, sin_ref, w1_ref, wg_ref, wq_ref, wuk_ref, qn_ref, kvn_ref,
                 glu_ref, ckv_ref, kpe_ref, kcat_ref, qcat_ref, sq_ref, sk_ref, sv_ref, skv_ref,
                 mqh_ref, mql_ref, mk_ref, mv_ref, mkx_ref, mvb_ref, gate_ref, *, tm, t_p):
    x = x_ref[...].astype(BF16)
    u = _dot(x, w1_ref[...])
    gate_ref[...] = _dot(x, wg_ref[...])
    cos = cos_ref[...]
    sin = sin_ref[...]
    glu_ref[...] = u[:, C_CONV:C_CONV + CONV_CH] * jax.nn.sigmoid(u[:, C_CONV + CONV_CH:C_CQ])
    ckvn = _rms_norm(u[:, C_CKV:C_SQ], kvn_ref[...])
    ckv_ref[...] = ckvn
    kpe = u[:, C_KPE:C_KROT] * cos + u[:, C_KROT:C_END] * sin
    kpe_ref[...] = kpe[:, :MLA_ROPE]
    kcat_ref[...] = jnp.concatenate([ckvn, kpe], axis=-1).astype(BF16)
    cqn = _rms_norm(u[:, C_CQ:C_CKV], qn_ref[...])
    qa = _dot(cqn.astype(BF16), wq_ref[...])
    nq = MLA_HEADS * MLA_NOPE
    qlat = _dot(qa[:, :nq].astype(BF16), wuk_ref[...])
    parts = []
    for h in range(MLA_HEADS):
        pe = (qa[:, nq + LANE * h:nq + LANE * (h + 1)] * cos
              + qa[:, nq + LANE * (MLA_HEADS + h):nq + LANE * (MLA_HEADS + h + 1)] * sin)
        parts += [qlat[:, MLA_KV_LORA * h:MLA_KV_LORA * (h + 1)], pe]
    qcat_ref[...] = jnp.concatenate(parts, axis=-1).astype(BF16)
    sq_ref[...] = (u[:, C_SQ:C_SK] * QK_SCALE).astype(BF16)
    sk = u[:, C_SK:C_SV]
    sv = u[:, C_SV:C_MQ]
    sk_ref[...] = sk
    sv_ref[...] = sv
    skv_ref[...] = jnp.concatenate([sk, sv], axis=-1).astype(BF16)
    mqh, mql = _split_bf16(u[:, C_MQ:C_MK] * QK_SCALE)
    mqh_ref[...] = mqh
    mql_ref[...] = mql
    mk = u[:, C_MK:C_MV]
    mv = u[:, C_MV:C_KPE]
    mk_ref[...] = mk
    mv_ref[...] = mv
    mvb_ref[...] = mv.astype(BF16)
    row = pl.program_id(0) * tm + lax.broadcasted_iota(jnp.int32, (tm, LANE), 0)
    lane = lax.broadcasted_iota(jnp.int32, (tm, LANE), 1)
    blk = row // MB_BLOCK
    onehot = jnp.where(lane == blk, 1.0, jnp.where(lane == blk + HALF_LANE, 1.0, 0.0))
    onehot = jnp.where(row < t_p, onehot, 0.0)
    mkx_ref[...] = jnp.concatenate([mk, onehot], axis=-1).astype(BF16)


def _project(x, cos, sin, w1, wg, wq, wuk, qn, kvn, t_p):
    n = x.shape[0]
    tm = _pick_tile(n, (256, 128, 64, 32, 16, 8))
    row = lambda w: pl.BlockSpec((tm, w), lambda i: (i, 0))
    outs = [("glu", CONV_CH, F32), ("ckv", MLA_KV_LORA, F32), ("kpe", MLA_ROPE, F32),
            ("kcat", 2 * LANE, BF16), ("qcat", MLA_HEADS * 2 * LANE, BF16),
            ("sq", N_HEADS * LANE, BF16), ("sk", LANE, F32), ("sv", LANE, F32), ("skv", 2 * LANE, BF16),
            ("mqh", N_HEADS * LANE, BF16), ("mql", N_HEADS * LANE, BF16),
            ("mk", LANE, F32), ("mv", LANE, F32), ("mkx", 2 * LANE, BF16), ("mvb", LANE, BF16),
            ("gate", N_BRANCH * D_MODEL, F32)]
    res = pl.pallas_call(
        functools.partial(_proj_kernel, tm=tm, t_p=t_p),
        grid=(n // tm,),
        in_specs=[row(D_MODEL), row(LANE), row(LANE), _full(w1.shape), _full(wg.shape), _full(wq.shape),
                  _full(wuk.shape), _full(qn.shape), _full(kvn.shape)],
        out_specs=[row(w) for _, w, _ in outs],
        out_shape=[jax.ShapeDtypeStruct((n, w), dt) for _, w, dt in outs],
        compiler_params=_cparams("parallel"),
        name="proj",
    )(x, cos, sin, w1, wg, wq, wuk, qn, kvn)
    return {name: r for (name, _, _), r in zip(outs, res)}


def _causal_ids(tq):
    row = lax.broadcasted_iota(jnp.int32, (tq, tq), 0)
    col = lax.broadcasted_iota(jnp.int32, (tq, tq), 1)
    return row, col


def _flash_init(m_ref, acc_ref):
    m_ref[...] = jnp.full(m_ref.shape, NEG, F32)
    acc_ref[...] = jnp.zeros(acc_ref.shape, F32)


def _flash_update(h, s, v_ones, m_ref, acc_ref, base2=False):
    ex = jnp.exp2 if base2 else jnp.exp
    m_prev = m_ref[h]
    m_new = jnp.maximum(m_prev, jnp.max(s, axis=-1, keepdims=True))
    p = ex(s - jnp.tile(m_new, (1, s.shape[1] // LANE)))
    alpha = ex(m_prev - m_new)
    acc_ref[h] = jnp.tile(alpha, (1, 2)) * acc_ref[h] + _dot(p.astype(BF16), v_ones)
    m_ref[h] = m_new


BLOCKS_PER_TRIP = 4


def _unrolled_blocks(n, step):
    u = BLOCKS_PER_TRIP

    def body(t, carry):
        for k in range(u):
            step(u * t + k)
        return carry

    lax.fori_loop(0, n // u, body, 0)
    for k in range(u - 1):
        @pl.when(n % u > k)
        def _():
            step((n // u) * u + k)


def _flash_result(h, acc_ref):
    acc = acc_ref[h]
    return acc[:, :LANE] / acc[:, LANE:]


def _mla_prompt_kernel(q_ref, k_ref, o_ref, m_ref, acc_ref, *, tq):
    i = pl.program_id(0)
    _flash_init(m_ref, acc_ref)
    ones = jnp.ones((tq, LANE), BF16)

    def step(j, masked):
        k = k_ref[pl.ds(pl.multiple_of(j * tq, tq), tq), :]
        v_ones = jnp.concatenate([k[:, :MLA_KV_LORA], ones], axis=-1)
        for h in range(MLA_HEADS):
            s = _dot_nt(q_ref[:, h * 2 * LANE:(h + 1) * 2 * LANE], k) * (MLA_SCALE * LOG2_E)
            if masked:
                row, col = _causal_ids(tq)
                s = jnp.where(col <= row, s, NEG)
            _flash_update(h, s, v_ones, m_ref, acc_ref, base2=True)

    _unrolled_blocks(i, lambda j: step(j, False))
    step(i, True)
    for h in range(MLA_HEADS):
        o_ref[:, h * MLA_KV_LORA:(h + 1) * MLA_KV_LORA] = _flash_result(h, acc_ref)


def _mla_prompt(qcat, kcat, t_p, tq):
    return pl.pallas_call(
        functools.partial(_mla_prompt_kernel, tq=tq),
        grid=(t_p // tq,),
        in_specs=[pl.BlockSpec((tq, MLA_HEADS * 2 * LANE), lambda i: (i, 0)),
                  pl.BlockSpec((t_p, 2 * LANE), lambda i: (0, 0))],
        out_specs=pl.BlockSpec((tq, MLA_HEADS * MLA_KV_LORA), lambda i: (i, 0)),
        out_shape=jax.ShapeDtypeStruct((t_p, MLA_HEADS * MLA_KV_LORA), F32),
        scratch_shapes=[pltpu.VMEM((MLA_HEADS, tq, LANE), F32), pltpu.VMEM((MLA_HEADS, tq, 2 * LANE), F32)],
        compiler_params=_cparams("parallel"),
        name="mla_prompt",
    )(qcat, kcat)


def _log_sigmoid_pair(z):
    ls = jnp.minimum(z, 0.0) - jnp.log(1.0 + jnp.exp(-jnp.abs(z)))
    return ls, ls - z


def _suffix_sum(l1m, uu):
    hi, lo = _split_bf16(l1m)
    return _dot(jnp.concatenate([hi, lo], axis=-1), uu)


def _group_lanes(x, h):
    lane = lax.broadcasted_iota(jnp.int32, x.shape, 1)
    g = h // (N_HEADS // KV_HEADS)
    return jnp.where((lane >= HEAD_DIM * g) & (lane < HEAD_DIM * (g + 1)), x, 0.0)


def _sb_prompt_kernel(q_ref, kv_ref, uu_ref, o_ref, qs_ref, c_ref, acc_ref, *, tq):
    i = pl.program_id(0)
    rows = N_HEADS * tq
    for h in range(N_HEADS):
        qs_ref[h * tq:(h + 1) * tq, :] = q_ref[:, h * LANE:(h + 1) * LANE]
    c_ref[...] = jnp.zeros(c_ref.shape, F32)
    acc_ref[...] = jnp.zeros(acc_ref.shape, F32)

    def step(j, masked):
        kv = kv_ref[pl.ds(pl.multiple_of(j * tq, tq), tq), :]
        z = _dot_nt(qs_ref[...], kv[:, :LANE])
        ls, l1m = _log_sigmoid_pair(z)
        if masked:
            row = lax.broadcasted_iota(jnp.int32, (rows, tq), 0) & (tq - 1)
            col = lax.broadcasted_iota(jnp.int32, (rows, tq), 1)
            mask = col < row
            l1m = jnp.where(mask, l1m, 0.0)
        a = jnp.exp(ls + _suffix_sum(l1m, uu_ref[...]) + c_ref[...])
        if masked:
            a = jnp.where(mask, a, 0.0)
        acc_ref[...] += _dot(a.astype(BF16), kv[:, LANE:])
        c_ref[...] += jnp.sum(l1m, axis=-1, keepdims=True)

    step(i, True)

    def body(t, carry):
        step(i - 1 - 2 * t, False)
        step(i - 2 - 2 * t, False)
        return carry

    lax.fori_loop(0, i // 2, body, 0)

    @pl.when(i % 2 == 1)
    def _():
        step(0, False)

    acc = acc_ref[...]
    for h in range(N_HEADS):
        o_ref[:, h * LANE:(h + 1) * LANE] = _group_lanes(acc[h * tq:(h + 1) * tq], h)


def _suffix_matrix(tk):
    u = (np.arange(tk)[:, None] > np.arange(tk)[None, :]).astype(np.float32)
    return jnp.asarray(np.concatenate([u, u], axis=0), dtype=BF16)


def _sb_prompt(sq, skv, t_p, tq):
    return pl.pallas_call(
        functools.partial(_sb_prompt_kernel, tq=tq),
        grid=(t_p // tq,),
        in_specs=[pl.BlockSpec((tq, N_HEADS * LANE), lambda i: (i, 0)),
                  pl.BlockSpec((t_p, 2 * LANE), lambda i: (0, 0)),
                  _full((2 * tq, tq))],
        out_specs=pl.BlockSpec((tq, N_HEADS * LANE), lambda i: (i, 0)),
        out_shape=jax.ShapeDtypeStruct((t_p, N_HEADS * LANE), F32),
        scratch_shapes=[pltpu.VMEM((N_HEADS * tq, LANE), BF16), pltpu.VMEM((N_HEADS * tq, 1), F32),
                        pltpu.VMEM((N_HEADS * tq, LANE), F32)],
        compiler_params=_cparams("parallel"),
        name="sb_prompt",
    )(sq, skv, _suffix_matrix(tq))


def _top3_select(score, valid):
    lane = lax.broadcasted_iota(jnp.int32, score.shape, 1).astype(F32)
    s = jnp.where(valid, score, -jnp.inf)
    sel = jnp.zeros(score.shape, F32)
    for _ in range(MB_TOPK):
        mx = jnp.max(s, axis=-1, keepdims=True)
        cand = jnp.where(s == mx, lane, 1e9)
        cand = jnp.where(mx > -jnp.inf, cand, 1e9)
        pick = lane == jnp.min(cand, axis=-1, keepdims=True)
        sel = jnp.where(pick, 1.0, sel)
        s = jnp.where(pick, -jnp.inf, s)
    return sel


def _block_scores(qh, ql, kmean):
    kh, kl = _split_bf16(kmean)
    return _dot_nt(qh, kh) + _dot_nt(qh, kl) + _dot_nt(ql, kh)


def _moba_prompt_kernel(t5_ref, qh_ref, ql_ref, kx_ref, v_ref, kmean_ref, bias_ref, o_ref,
                        qx_ref, m_ref, acc_ref, *, tq):
    i = pl.program_id(0)
    _flash_init(m_ref, acc_ref)
    ones = jnp.ones((tq, LANE), BF16)
    lane = lax.broadcasted_iota(jnp.int32, (tq, LANE), 1)
    kmean = kmean_ref[...]
    for h in range(N_HEADS):
        qh = qh_ref[:, h * LANE:(h + 1) * LANE]
        sel = _top3_select(_block_scores(qh, ql_ref[:, h * LANE:(h + 1) * LANE], kmean), lane < i)
        far_hi, far_lo = _split_bf16(jnp.full((tq, LANE), t5_ref[T5_BUCKETS - 1, h], F32))
        hi = jnp.where(sel > 0.0, far_hi.astype(F32), NEG)
        hi = jnp.where(lane == i - 1, jnp.where(sel > 0.0, 0.0, NEG), hi)
        hi = jnp.where(lane == i, 0.0, hi)
        sel_up = pltpu.roll(sel, HALF_LANE, axis=1)
        lo = jnp.where((sel_up > 0.0) & (lane - HALF_LANE < i - 1), far_lo.astype(F32), 0.0)
        qbias = jnp.where(lane < HALF_LANE, hi, lo)
        qx_ref[h] = jnp.concatenate([qh, qbias.astype(BF16)], axis=-1)

    def step(j, near):
        rows = pl.ds(pl.multiple_of(j * tq, tq), tq)
        k = kx_ref[rows, :]
        v_ones = jnp.concatenate([v_ref[rows, :], ones], axis=-1)
        for h in range(N_HEADS):
            s = _dot_nt(qx_ref[h], k)
            if near == "own":
                row, col = _causal_ids(tq)
                s = jnp.where(col <= row, s + bias_ref[h * tq:(h + 1) * tq, tq:], NEG)
            elif near == "previous":
                s = s + bias_ref[h * tq:(h + 1) * tq, :tq]
            _flash_update(h, s, v_ones, m_ref, acc_ref)

    step(i, "own")

    @pl.when(i >= 1)
    def _():
        step(i - 1, "previous")

    _unrolled_blocks(jnp.maximum(i - 1, 0), lambda j: step(j, None))
    for h in range(N_HEADS):
        o_ref[:, h * LANE:(h + 1) * LANE] = _group_lanes(_flash_result(h, acc_ref), h)


def _moba_prompt(t5_bias, mqh, mql, mkx, mvb, kmean, bias, t_p):
    tq = MB_BLOCK
    assert t_p // MB_BLOCK <= HALF_LANE
    qspec = pl.BlockSpec((tq, N_HEADS * LANE), lambda i: (i, 0))
    return pl.pallas_call(
        functools.partial(_moba_prompt_kernel, tq=tq),
        grid=(t_p // tq,),
        in_specs=[pl.BlockSpec(memory_space=pltpu.SMEM), qspec, qspec,
                  pl.BlockSpec((t_p, 2 * LANE), lambda i: (0, 0)),
                  pl.BlockSpec((t_p, LANE), lambda i: (0, 0)),
                  _full(kmean.shape), _full(bias.shape)],
        out_specs=pl.BlockSpec((tq, N_HEADS * LANE), lambda i: (i, 0)),
        out_shape=jax.ShapeDtypeStruct((t_p, N_HEADS * LANE), F32),
        scratch_shapes=[pltpu.VMEM((N_HEADS, tq, 2 * LANE), BF16), pltpu.VMEM((N_HEADS, tq, LANE), F32),
                        pltpu.VMEM((N_HEADS, tq, 2 * LANE), F32)],
        compiler_params=_cparams("parallel"),
        name="moba_prompt",
    )(t5_bias, mqh, mql, mkx, mvb, kmean, bias)


def _kmean_kernel(k_ref, o_ref):
    k = k_ref[...]
    nb = k.shape[0] // MB_BLOCK
    o_ref[...] = jnp.sum(k.reshape(nb, MB_BLOCK, LANE), axis=1) * (1.0 / MB_BLOCK)


def _block_means(mk, t_p):
    nb = t_p // MB_BLOCK
    per = _pick_tile(nb, (8,))
    return pl.pallas_call(
        _kmean_kernel,
        grid=(nb // per,),
        in_specs=[pl.BlockSpec((per * MB_BLOCK, LANE), lambda i: (i, 0))],
        out_specs=pl.BlockSpec((per, LANE), lambda i: (i, 0)),
        out_shape=jax.ShapeDtypeStruct((nb, LANE), F32),
        compiler_params=_cparams("parallel"),
        name="moba_kmean",
    )(mk)


def _t5_kernel(t5_ref, bucket_ref, o_ref):
    bucket = bucket_ref[...]
    for h in range(N_HEADS):
        acc = jnp.zeros(bucket.shape, F32)
        for b in range(T5_BUCKETS):
            acc = jnp.where(bucket == b, t5_ref[b, h], acc)
        o_ref[h] = acc


def _t5_tiles(t5_bias, bucket):
    return pl.pallas_call(
        _t5_kernel,
        in_specs=[pl.BlockSpec(memory_space=pltpu.SMEM), pl.BlockSpec(memory_space=pltpu.VMEM)],
        out_specs=pl.BlockSpec(memory_space=pltpu.VMEM),
        out_shape=jax.ShapeDtypeStruct((N_HEADS,) + bucket.shape, F32),
        name="t5_tiles",
    )(t5_bias, bucket)


CHUNK_PAGES = 16
RING_SLOTS = 8
LOOKAHEAD = RING_SLOTS - 1
SAMPLE_ROWS = 4 * N_HEADS


class _PageStream:
    def __init__(self, pt_ref, layer, caches, rings, sem_ref, n_chunks):
        self.pt_ref, self.layer, self.caches, self.rings = pt_ref, layer, caches, rings
        self.sem_ref, self.n_chunks = sem_ref, n_chunks

    def _copies(self, g):
        slot = g % RING_SLOTS
        for r in range(CHUNK_PAGES):
            page = self.pt_ref[g * CHUNK_PAGES + r]
            for which, (cache, ring) in enumerate(zip(self.caches, self.rings)):
                yield pltpu.make_async_copy(cache.at[self.layer, page], ring.at[slot, r],
                                            self.sem_ref.at[slot, which])

    def start(self, g):
        for cp in self._copies(g):
            cp.start()

    def wait(self, g):
        for cp in self._copies(g):
            cp.wait()

    def consume(self, b, chunks_per_seq, use):
        @pl.when(b == 0)
        def _():
            for g in range(LOOKAHEAD):
                self.start(g)

        for k in range(chunks_per_seq):
            g = b * chunks_per_seq + k
            self.wait(g)

            @pl.when(g + LOOKAHEAD < self.n_chunks)
            def _():
                self.start(g + LOOKAHEAD)

            use(k, g % RING_SLOTS)


def _ring(cache):
    return pltpu.VMEM((RING_SLOTS, CHUNK_PAGES) + cache.shape[2:], cache.dtype)


def _feature_major(cache):
    l, pool, page = cache.shape[:3]
    perm = (0, 1) + tuple(range(3, cache.ndim)) + (2,)
    return cache.transpose(perm).reshape(l, pool, -1, page)


def _sample_tokens():
    return lax.broadcasted_iota(jnp.int32, (SAMPLE_ROWS, PAGE_SIZE), 0) // N_HEADS


def _own_group_lanes(x):
    rows = x.shape[0]
    lane = lax.broadcasted_iota(jnp.int32, (rows, LANE), 1)
    g = (lax.broadcasted_iota(jnp.int32, (rows, LANE), 0) % N_HEADS) // (N_HEADS // KV_HEADS)
    return jnp.where((lane >= HEAD_DIM * g) & (lane < HEAD_DIM * (g + 1)), x, 0.0)


def _mla_sample_kernel(pt_ref, q_ref, knew_ref, ckv_hbm, kpe_hbm, o_ref, s_ref, kbuf_ref,
                       ckv_ring, kpe_ring, sem_ref, *, n_pages, layer, n_chunks):
    q = q_ref[...]
    q_lat = q[:, :MLA_KV_LORA]
    q_pe = q[:, MLA_KV_LORA:MLA_KV_LORA + MLA_ROPE]

    def use(k, slot):
        for r in range(0, CHUNK_PAGES, 2):
            pg = k * CHUNK_PAGES + r
            ckv = jnp.concatenate([ckv_ring[slot, r], ckv_ring[slot, r + 1]], axis=0).astype(BF16)
            kpe_t = jnp.concatenate([kpe_ring[slot, r], kpe_ring[slot, r + 1]], axis=1).astype(BF16)
            kbuf_ref[pg // 2] = ckv
            s = (_dot_nt(q_lat, ckv) + _dot(q_pe, kpe_t)) * MLA_SCALE
            s_ref[pg] = s[:, :PAGE_SIZE]
            s_ref[pg + 1] = s[:, PAGE_SIZE:]

    stream = _PageStream(pt_ref, layer, (ckv_hbm, kpe_hbm), (ckv_ring, kpe_ring), sem_ref, n_chunks)
    stream.consume(pl.program_id(0), n_pages // CHUNK_PAGES, use)

    kn = knew_ref[...]
    s_new = _dot_nt(q, kn) * MLA_SCALE
    col = lax.broadcasted_iota(jnp.int32, s_new.shape, 1)
    s_new = jnp.where(col <= _sample_tokens(), s_new, NEG)
    s_all = s_ref[...]
    m = jnp.maximum(jnp.max(jnp.max(s_all, axis=0), axis=-1, keepdims=True),
                    jnp.max(s_new, axis=-1, keepdims=True))
    p_new = jnp.exp(s_new - m)
    p_all = jnp.exp(s_all - m)
    l = (jnp.sum(p_new, axis=-1, keepdims=True)
         + jnp.sum(jnp.sum(p_all, axis=0), axis=-1, keepdims=True))
    acc = _dot(p_new.astype(BF16), kn[:, :MLA_KV_LORA])
    for j in range(n_pages // 2):
        p2 = jnp.concatenate([p_all[2 * j], p_all[2 * j + 1]], axis=1)
        acc = acc + _dot(p2.astype(BF16), kbuf_ref[j])
    o_ref[...] = acc / l


def _sample_call(kernel, name, pt, layer, inputs, in_specs, caches, out_width, scratch_shapes):
    b = inputs[0].shape[0] // SAMPLE_ROWS
    n_pages = pt.shape[0] // b
    assert n_pages % CHUNK_PAGES == 0
    n_chunks = b * n_pages // CHUNK_PAGES
    assert n_chunks >= LOOKAHEAD
    grid_spec = pltpu.PrefetchScalarGridSpec(
        num_scalar_prefetch=1, grid=(b,),
        in_specs=in_specs + [pl.BlockSpec(memory_space=pl.ANY)] * len(caches),
        out_specs=_row_spec(out_width),
        scratch_shapes=scratch_shapes + [_ring(c) for c in caches]
        + [pltpu.SemaphoreType.DMA((RING_SLOTS, len(caches)))])
    return pl.pallas_call(
        functools.partial(kernel, n_pages=n_pages, layer=layer, n_chunks=n_chunks),
        grid_spec=grid_spec,
        out_shape=jax.ShapeDtypeStruct((b * SAMPLE_ROWS, out_width), F32),
        compiler_params=_cparams("arbitrary"),
        name=name,
    )(pt, *inputs, *caches)


def _row_spec(width):
    return pl.BlockSpec((SAMPLE_ROWS, width), lambda i, pt: (i, 0))


def _new_spec(width):
    return pl.BlockSpec((None, PAGE_SIZE, width), lambda i, pt: (i, 0, 0))


def _mla_sample(pt, q, knew, cache_ckv, cache_kpe, layer):
    n_pages = pt.shape[0] // (q.shape[0] // SAMPLE_ROWS)
    return _sample_call(
        _mla_sample_kernel, "mla_sample", pt, layer, [q, knew],
        [_row_spec(2 * LANE), _new_spec(2 * LANE)], [cache_ckv, cache_kpe], MLA_KV_LORA,
        [pltpu.VMEM((n_pages, SAMPLE_ROWS, PAGE_SIZE), F32),
         pltpu.VMEM((n_pages // 2, 2 * PAGE_SIZE, MLA_KV_LORA), BF16)])


def _sb_sample_kernel(pt_ref, q_ref, knew_ref, vnew_ref, uu_ref, k_hbm, v_hbm, o_ref, z_ref, vbuf_ref,
                      k_ring, v_ring, sem_ref, *, n_pages, layer, n_chunks):
    q = q_ref[...]

    def use(k, slot):
        for r in range(0, CHUNK_PAGES, 2):
            pg = k * CHUNK_PAGES + r
            kk = jnp.concatenate([k_ring[slot, r], k_ring[slot, r + 1]], axis=1)
            z = _dot(q, kk.astype(BF16))
            z_ref[pg] = z[:, :PAGE_SIZE]
            z_ref[pg + 1] = z[:, PAGE_SIZE:]
            vbuf_ref[pg // 2] = jnp.concatenate([v_ring[slot, r], v_ring[slot, r + 1]], axis=1).astype(BF16)

    stream = _PageStream(pt_ref, layer, (k_hbm, v_hbm), (k_ring, v_ring), sem_ref, n_chunks)
    stream.consume(pl.program_id(0), n_pages // CHUNK_PAGES, use)

    uu = uu_ref[...]
    ls, l1m = _log_sigmoid_pair(_dot_nt(q, knew_ref[...]))
    col = lax.broadcasted_iota(jnp.int32, ls.shape, 1)
    mask = col < _sample_tokens()
    l1m = jnp.where(mask, l1m, 0.0)
    within = _suffix_sum(l1m, uu)
    a = jnp.where(mask, jnp.exp(ls + within), 0.0)
    acc = _dot(a.astype(BF16), vnew_ref[...])
    carry = within[:, 0:1] + l1m[:, 0:1]
    ls, l1m = _log_sigmoid_pair(z_ref[...])
    within = _suffix_sum(l1m.reshape(n_pages * SAMPLE_ROWS, PAGE_SIZE), uu)
    within = within.reshape(n_pages, SAMPLE_ROWS, PAGE_SIZE)
    for j in reversed(range(n_pages // 2)):
        pair = []
        for pg in (2 * j + 1, 2 * j):
            pair.append(jnp.exp(ls[pg] + within[pg] + carry))
            carry = carry + within[pg][:, 0:1] + l1m[pg][:, 0:1]
        a2 = jnp.concatenate([pair[1], pair[0]], axis=1)
        acc = acc + _dot_nt(a2.astype(BF16), vbuf_ref[j])
    o_ref[...] = _own_group_lanes(acc)


def _sb_sample(pt, q, knew, vnew, cache_k, cache_v, layer):
    n_pages = pt.shape[0] // (q.shape[0] // SAMPLE_ROWS)
    return _sample_call(
        _sb_sample_kernel, "sb_sample", pt, layer, [q, knew, vnew, _suffix_matrix(PAGE_SIZE)],
        [_row_spec(LANE), _new_spec(LANE), _new_spec(LANE),
         pl.BlockSpec((2 * PAGE_SIZE, PAGE_SIZE), lambda i, pt: (0, 0))],
        [cache_k, cache_v], LANE,
        [pltpu.VMEM((n_pages, SAMPLE_ROWS, PAGE_SIZE), F32),
         pltpu.VMEM((n_pages // 2, LANE, 2 * PAGE_SIZE), BF16)])


def _moba_sample_kernel(pt_ref, qh_ref, ql_ref, knew_ref, vnew_ref, bias_ref, far_ref, k_hbm, v_hbm,
                        o_ref, s_ref, vbuf_ref, kmean_ref, k_ring, v_ring, sem_ref,
                        *, n_pages, layer, n_chunks):
    qh = qh_ref[...]
    pages_per_block = MB_BLOCK // PAGE_SIZE
    n_blocks = n_pages // pages_per_block
    kmean_ref[...] = jnp.zeros(kmean_ref.shape, F32)
    blk_lane = lax.broadcasted_iota(jnp.int32, kmean_ref.shape, 1)

    def use(k, slot):
        for r in range(0, CHUNK_PAGES, pages_per_block):
            pg = k * CHUNK_PAGES + r
            kk = jnp.concatenate([k_ring[slot, r + j] for j in range(pages_per_block)], axis=1)
            s = _dot(qh, kk.astype(BF16))
            for j in range(pages_per_block):
                s_ref[pg + j] = s[:, j * PAGE_SIZE:(j + 1) * PAGE_SIZE]
            vbuf_ref[pg // pages_per_block] = jnp.concatenate(
                [v_ring[slot, r + j] for j in range(pages_per_block)], axis=1).astype(BF16)
            ksum = jnp.sum(kk, axis=1, keepdims=True)
            kmean_ref[...] += jnp.where(blk_lane == pg // pages_per_block, ksum * (1.0 / MB_BLOCK), 0.0)

    stream = _PageStream(pt_ref, layer, (k_hbm, v_hbm), (k_ring, v_ring), sem_ref, n_chunks)
    stream.consume(pl.program_id(0), n_pages // CHUNK_PAGES, use)

    kh, kl = _split_bf16(kmean_ref[...])
    score = _dot(qh, kh) + _dot(qh, kl) + _dot(ql_ref[...], kh)
    blk = lax.broadcasted_iota(jnp.int32, score.shape, 1)
    sel = _top3_select(score, blk < n_blocks)
    far_bias = far_ref[...]
    bias = bias_ref[...]
    s_new = _dot_nt(qh, knew_ref[...]) + bias[:, MB_BLOCK:MB_BLOCK + PAGE_SIZE]
    col = lax.broadcasted_iota(jnp.int32, s_new.shape, 1)
    s_new = jnp.where(col <= _sample_tokens(), s_new, NEG)
    m_lanes = s_new
    masked = []
    for n in range(n_blocks):
        keep = sel[:, n:n + 1] > 0.0
        for pg in range(n * pages_per_block, (n + 1) * pages_per_block):
            if n == n_blocks - 1:
                off = (pg - n * pages_per_block) * PAGE_SIZE
                s = s_ref[pg] + bias[:, off:off + PAGE_SIZE]
            else:
                s = s_ref[pg] + far_bias
            s = jnp.where(keep, s, NEG)
            masked.append(s)
            m_lanes = jnp.maximum(m_lanes, s)
    m = jnp.max(m_lanes, axis=-1, keepdims=True)
    p_new = jnp.exp(s_new - m)
    l_lanes = p_new
    acc = _dot(p_new.astype(BF16), vnew_ref[...])
    for n in range(n_blocks):
        ps = [jnp.exp(masked[n * pages_per_block + j] - m) for j in range(pages_per_block)]
        for p in ps:
            l_lanes = l_lanes + p
        acc = acc + _dot_nt(jnp.concatenate(ps, axis=1).astype(BF16), vbuf_ref[n])
    o_ref[...] = _own_group_lanes(acc / jnp.sum(l_lanes, axis=-1, keepdims=True))


def _moba_sample(pt, qh, ql, knew, vnew, bias, far_bias, cache_k, cache_v, layer, nbp):
    n_pages = pt.shape[0] // (qh.shape[0] // SAMPLE_ROWS)
    return _sample_call(
        _moba_sample_kernel, "moba_sample", pt, layer, [qh, ql, knew, vnew, bias, far_bias],
        [_row_spec(LANE), _row_spec(LANE), _new_spec(LANE), _new_spec(LANE),
         pl.BlockSpec(bias.shape, lambda i, pt: (0, 0)), pl.BlockSpec(far_bias.shape, lambda i, pt: (0, 0))],
        [cache_k, cache_v], LANE,
        [pltpu.VMEM((n_pages, SAMPLE_ROWS, PAGE_SIZE), F32),
         pltpu.VMEM((n_pages * PAGE_SIZE // MB_BLOCK, LANE, MB_BLOCK), BF16),
         pltpu.VMEM((LANE, nbp), F32)])


CONV_HALO = 32


def _conv_prompt_kernel(cur_ref, prev_ref, w_ref, b_ref, g_ref, beta_ref, o_ref, ext_ref, *, tm):
    i = pl.program_id(0)
    prev = prev_ref[...]
    ext_ref[:CONV_HALO, :] = jnp.where(i > 0, prev, 0.0)
    ext_ref[CONV_HALO:, :] = cur_ref[...]
    acc = jnp.zeros((tm, CONV_CH), F32)
    base = CONV_HALO - (CONV_WIDTH - 1)
    for w in range(CONV_WIDTH):
        acc = acc + ext_ref[base + w:base + w + tm, :] * w_ref[w:w + 1, :]
    o_ref[...] = _silu(_layer_norm(acc + b_ref[...], g_ref[...], beta_ref[...]))


def _conv_prompt(glu, t_p, conv_w, conv_b, ln_g, ln_b):
    tm = _pick_tile(t_p, (512, 256, 128, 64, 32))
    per = tm // CONV_HALO
    return pl.pallas_call(
        functools.partial(_conv_prompt_kernel, tm=tm),
        grid=(t_p // tm,),
        in_specs=[pl.BlockSpec((tm, CONV_CH), lambda i: (i, 0)),
                  pl.BlockSpec((CONV_HALO, CONV_CH), lambda i: (jnp.maximum(i * per - 1, 0), 0)),
                  _full(conv_w.shape), _full(conv_b.shape), _full(ln_g.shape), _full(ln_b.shape)],
        out_specs=pl.BlockSpec((tm, CONV_CH), lambda i: (i, 0)),
        out_shape=jax.ShapeDtypeStruct((t_p, CONV_CH), F32),
        scratch_shapes=[pltpu.VMEM((tm + CONV_HALO, CONV_CH), F32)],
        compiler_params=_cparams("parallel"),
        name="conv_prompt",
    )(glu, glu, conv_w, conv_b, ln_g, ln_b)


def _conv_sample_kernel(ext_ref, w_ref, b_ref, g_ref, beta_ref, o_ref):
    n_new = o_ref.shape[0]
    for t in range(n_new):
        acc = jnp.zeros(o_ref.shape[1:], F32)
        for w in range(CONV_WIDTH):
            acc = acc + ext_ref[t + w] * w_ref[w:w + 1, :]
        o_ref[t] = _silu(_layer_norm(acc + b_ref[...], g_ref[...], beta_ref[...]))


def _conv_sample(ext_t, conv_w, conv_b, ln_g, ln_b):
    n_new = ext_t.shape[0] - (CONV_WIDTH - 1)
    vm = pl.BlockSpec(memory_space=pltpu.VMEM)
    return pl.pallas_call(
        _conv_sample_kernel,
        in_specs=[vm] * 5,
        out_specs=vm,
        out_shape=jax.ShapeDtypeStruct((n_new,) + ext_t.shape[1:], F32),
        name="conv_sample",
    )(ext_t, conv_w, conv_b, ln_g, ln_b)


def _merge_kernel(x_ref, cact_ref, lat_ref, sb_ref, mb_ref, gate_ref, gb_ref, wc_ref, wuv_ref, wm_ref,
                  ws_ref, wo_ref, wout_ref, g_ref, b_ref, o_ref, *, alpha):
    c = _dot(cact_ref[...].astype(BF16), wc_ref[...])
    mv = _dot(lat_ref[...].astype(BF16), wuv_ref[...])
    m = _dot(mv.astype(BF16), wm_ref[...])
    s = _dot(sb_ref[...].astype(BF16), ws_ref[...])
    o = _dot(mb_ref[...].astype(BF16), wo_ref[...])
    g = jax.nn.sigmoid(gate_ref[...] + gb_ref[...])
    d = D_MODEL
    y = g[:, :d] * c + g[:, d:2 * d] * m + g[:, 2 * d:3 * d] * s + g[:, 3 * d:] * o
    y = _dot(y.astype(BF16), wout_ref[...])
    o_ref[...] = _layer_norm(alpha * x_ref[...] + y, g_ref[...], b_ref[...])


def _merge(x, cact, lat, sb, mb, gate, gate_b, wc, wuv, wm, ws, wo, wout, ln_g, ln_b, alpha):
    n = x.shape[0]
    tm = _pick_tile(n, (256, 128, 64, 32, 16, 8))
    row = lambda w: pl.BlockSpec((tm, w), lambda i: (i, 0))
    weights = (gate_b, wc, wuv, wm, ws, wo, wout, ln_g, ln_b)
    return pl.pallas_call(
        functools.partial(_merge_kernel, alpha=alpha),
        grid=(n // tm,),
        in_specs=[row(a.shape[1]) for a in (x, cact, lat, sb, mb, gate)] + [_full(w.shape) for w in weights],
        out_specs=row(D_MODEL),
        out_shape=jax.ShapeDtypeStruct((n, D_MODEL), F32),
        compiler_params=_cparams("parallel"),
        name="merge",
    )(x, cact, lat, sb, mb, gate, *weights)


EXPERTS_PER_STEP = 4


def _first_index_of_max(s, idx):
    mx = jnp.max(s, axis=0, keepdims=True)
    first = jnp.min(jnp.where(s == mx, idx, 1e9), axis=0, keepdims=True)
    return idx == first


def _pick_column(x, j):
    lane = lax.broadcasted_iota(jnp.int32, x.shape, 1)
    return jnp.sum(jnp.where(lane == j, x, 0.0), axis=-1, keepdims=True)


def _router_kernel(x_ref, wh_ref, wl_ref, bias_ref, gate_ref, xb_ref):
    x = x_ref[...]
    xh, xl = _split_bf16(x)
    xb_ref[...] = xh
    logits = (_dot_nt(wh_ref[...], xh) + _dot_nt(wl_ref[...], xh) + _dot_nt(wh_ref[...], xl))[:N_EXPERTS]
    scores = jax.nn.sigmoid(logits)
    biased = scores + bias_ref[...]
    e, tm = biased.shape
    per = e // N_GROUPS
    grp = biased.reshape(N_GROUPS, per, tm)
    sub = lax.broadcasted_iota(jnp.int32, grp.shape, 1).astype(F32)
    top1 = jnp.max(grp, axis=1, keepdims=True)
    first = jnp.min(jnp.where(grp == top1, sub, 1e9), axis=1, keepdims=True)
    top2 = jnp.max(jnp.where(sub == first, -jnp.inf, grp), axis=1, keepdims=True)
    gs = jnp.broadcast_to(top1 + top2, grp.shape).reshape(e, tm)
    eidx = lax.broadcasted_iota(jnp.int32, biased.shape, 0)
    gidx = (eidx // per).astype(F32)
    e_keep = jnp.zeros(biased.shape, F32)
    for _ in range(TOPK_GROUPS):
        pick = _first_index_of_max(gs, gidx)
        e_keep = jnp.where(pick, 1.0, e_keep)
        gs = jnp.where(pick, -jnp.inf, gs)
    eidx = eidx.astype(F32)
    cand = jnp.where(e_keep > 0.0, biased, -jnp.inf)
    chosen = jnp.zeros(biased.shape, F32)
    for _ in range(TOP_K):
        pick = _first_index_of_max(cand, eidx)
        chosen = jnp.where(pick, 1.0, chosen)
        cand = jnp.where(pick, -jnp.inf, cand)
    sel = chosen * scores
    wts = sel / jnp.sum(sel, axis=0, keepdims=True) * ROUTED_SCALE
    gate_ref[...] = jnp.concatenate([wts, jnp.zeros_like(wts)], axis=0).T


def _router(x, router_w, router_bias):
    n = x.shape[0]
    tm = _pick_tile(n, (512, 256, 128))
    wt = jnp.pad(router_w.T, ((0, LANE - N_EXPERTS), (0, 0)))
    wh, wl = _split_bf16(wt)
    return pl.pallas_call(
        _router_kernel,
        grid=(n // tm,),
        in_specs=[pl.BlockSpec((tm, D_MODEL), lambda i: (i, 0)), _full(wh.shape), _full(wl.shape),
                  _full((N_EXPERTS, 1))],
        out_specs=[pl.BlockSpec((tm, LANE), lambda i: (i, 0)),
                   pl.BlockSpec((tm, D_MODEL), lambda i: (i, 0))],
        out_shape=[jax.ShapeDtypeStruct((n, LANE), F32), jax.ShapeDtypeStruct((n, D_MODEL), BF16)],
        compiler_params=_cparams("parallel"),
        name="moe_router",
    )(x, wh, wl, router_bias.reshape(N_EXPERTS, 1))


def _experts_kernel(x_ref, xb_ref, gate_ref, wg_ref, wu_ref, wd_ref, wsg_ref, wsu_ref, wsd_ref,
                    g_ref, b_ref, o_ref, acc_ref, *, alpha):
    step = pl.program_id(1)
    xb = xb_ref[...]

    @pl.when(step == 0)
    def _():
        h = _silu(_dot(xb, wsg_ref[...])) * _dot(xb, wsu_ref[...])
        acc_ref[...] = _dot(h.astype(BF16), wsd_ref[...])

    gate = gate_ref[...]
    hs = []
    for k in range(EXPERTS_PER_STEP):
        w = _pick_column(gate, step * EXPERTS_PER_STEP + k)
        h = _silu(_dot(xb, wg_ref[k])) * _dot(xb, wu_ref[k]) * w
        hs.append(h.astype(BF16))
    acc_ref[...] += _dot(jnp.concatenate(hs, axis=-1), wd_ref[...])

    @pl.when(step == pl.num_programs(1) - 1)
    def _():
        o_ref[...] = _layer_norm(alpha * x_ref[...] + acc_ref[...], g_ref[...], b_ref[...])


def _experts(x, xb, gate, wg, wu, wd, wsg, wsu, wsd, ln_g, ln_b, alpha):
    n = x.shape[0]
    tm = _pick_tile(n, (768, 512, 256, 128))
    n_exp, ff = wg.shape[0], wg.shape[2]
    wd = wd.reshape(n_exp // EXPERTS_PER_STEP, EXPERTS_PER_STEP * ff, D_MODEL)
    row = lambda w: pl.BlockSpec((tm, w), lambda i, e: (i, 0))
    return pl.pallas_call(
        functools.partial(_experts_kernel, alpha=alpha),
        grid=(n // tm, n_exp // EXPERTS_PER_STEP),
        in_specs=[row(D_MODEL), row(D_MODEL), row(LANE),
                  pl.BlockSpec((EXPERTS_PER_STEP, D_MODEL, ff), lambda i, e: (e, 0, 0)),
                  pl.BlockSpec((EXPERTS_PER_STEP, D_MODEL, ff), lambda i, e: (e, 0, 0)),
                  pl.BlockSpec((None, EXPERTS_PER_STEP * ff, D_MODEL), lambda i, e: (e, 0, 0)),
                  _full(wsg.shape), _full(wsu.shape), _full(wsd.shape), _full(ln_g.shape), _full(ln_b.shape)],
        out_specs=row(D_MODEL),
        out_shape=jax.ShapeDtypeStruct((n, D_MODEL), F32),
        scratch_shapes=[pltpu.VMEM((tm, D_MODEL), F32)],
        compiler_params=_cparams("parallel", "arbitrary"),
        name="moe_experts",
    )(x, xb, gate, wg, wu, wd, wsg, wsu, wsd, ln_g, ln_b)


def _vec(v):
    return v.reshape(1, -1)


def _pad_new(a):
    return jnp.pad(a, ((0, 0), (0, PAGE_SIZE - a.shape[1]), (0, 0))).astype(BF16)


def kernel(x_prompt, x_sample, cache_mla_ckv, cache_mla_kpe, cache_sb_k, cache_sb_v, cache_moba_k, cache_moba_v, state_conv, page_table, t5_bias, w_in, gate_b, conv_w, conv_b, conv_ln_g, conv_ln_b, w_conv_out, mla_q_norm, w_q_up, mla_kv_norm, w_kv_up, w_mla_out, w_sb_out, w_mb_out, w_out, ln1_g, ln1_b, router_w, router_bias, w_e_gate, w_e_up, w_e_down, w_s_gate, w_s_up, w_s_down, ln2_g, ln2_b):
    depth = w_in.shape[0]
    alpha = (2 * depth) ** 0.25
    bp, t_p, _ = x_prompt.shape
    assert bp == 1
    bs, n_new, _ = x_sample.shape
    n_s = bs * n_new
    n_pages = page_table.shape[1]
    past = n_pages * PAGE_SIZE
    assert n_new * N_HEADS == SAMPLE_ROWS
    assert t_p % MB_BLOCK == 0 and past % MB_BLOCK == 0

    x = jnp.concatenate([x_prompt.reshape(t_p, D_MODEL), x_sample.reshape(n_s, D_MODEL)], axis=0)
    pos = jnp.concatenate([jnp.arange(t_p, dtype=jnp.int32),
                           jnp.tile(past + jnp.arange(n_new, dtype=jnp.int32), bs)])
    cos, sin = _rope_tables(pos)
    pt = page_table.reshape(-1)
    kpe_t, sbk_t, sbv_t, mbk_t, mbv_t = (
        _feature_major(c) for c in (cache_mla_kpe, cache_sb_k, cache_sb_v, cache_moba_k, cache_moba_v))

    tq = MB_BLOCK
    rel_p = jnp.arange(tq)[:, None] - jnp.arange(2 * tq)[None, :] + tq
    bias_p = _t5_tiles(t5_bias, _t5_bucket(rel_p).astype(jnp.int32)).reshape(N_HEADS * tq, 2 * tq)
    kpos_s = jnp.concatenate([past - MB_BLOCK + jnp.arange(MB_BLOCK), past + jnp.arange(PAGE_SIZE)])
    rel_s = past + jnp.arange(8)[:, None] - kpos_s[None, :]
    bias_s = _t5_tiles(t5_bias, _t5_bucket(rel_s).astype(jnp.int32))
    bias_s = bias_s[:, :n_new].transpose(1, 0, 2).reshape(SAMPLE_ROWS, MB_BLOCK + PAGE_SIZE)
    far_s = jnp.tile(t5_bias[T5_BUCKETS - 1], n_new).reshape(SAMPLE_ROWS, 1)

    nb_p = t_p // MB_BLOCK
    nb_s = past // MB_BLOCK
    nbp_p = -(-nb_p // LANE) * LANE
    nbp_s = -(-nb_s // LANE) * LANE

    new_p = {k: [] for k in ("ckv", "kpe", "sk", "sv", "mk", "mv", "conv")}
    new_s = {k: [] for k in new_p}
    for l in range(depth):
        w1, wg, wq, wuk, wuv = _pack_layer_weights(w_in[l], w_q_up[l], w_kv_up[l])
        pc = _project(x, cos, sin, w1, wg, wq, wuk, _vec(mla_q_norm[l]), _vec(mla_kv_norm[l]), t_p)
        for k in ("ckv", "kpe", "sk", "sv", "mk", "mv"):
            new_p[k].append(pc[k][:t_p])
            new_s[k].append(pc[k][t_p:])

        lat_p = _mla_prompt(pc["qcat"], pc["kcat"], t_p, _pick_tile(t_p, (256, 128)))
        sb_p = _sb_prompt(pc["sq"], pc["skv"], t_p, _pick_tile(t_p, (256, 128)))
        kmean = jnp.pad(_block_means(pc["mk"], t_p), ((0, nbp_p - nb_p), (0, 0)))
        mb_p = _moba_prompt(t5_bias, pc["mqh"], pc["mql"], pc["mkx"], pc["mvb"], kmean, bias_p, t_p)
        glu_p = pc["glu"][:t_p]
        cact_p = _conv_prompt(pc["glu"], t_p, conv_w[l], _vec(conv_b[l]), _vec(conv_ln_g[l]), _vec(conv_ln_b[l]))
        new_p["conv"].append(glu_p[t_p - (CONV_WIDTH - 1):])

        def rows(a, width):
            return a[t_p:].reshape(n_s * N_HEADS, width)

        def newkeys(a):
            return _pad_new(a[t_p:].reshape(bs, n_new, a.shape[1]))

        lat_s = _mla_sample(pt, rows(pc["qcat"], 2 * LANE), newkeys(pc["kcat"]),
                            cache_mla_ckv, kpe_t, l)
        skv_new = newkeys(pc["skv"])
        sb_s = _sb_sample(pt, rows(pc["sq"], LANE), skv_new[..., :LANE], skv_new[..., LANE:], sbk_t, sbv_t, l)
        mb_s = _moba_sample(pt, rows(pc["mqh"], LANE), rows(pc["mql"], LANE),
                            newkeys(pc["mkx"])[..., :LANE], newkeys(pc["mvb"]), bias_s, far_s,
                            mbk_t, mbv_t, l, nbp_s)
        ext_t = jnp.concatenate([state_conv[l].transpose(1, 0, 2),
                                 pc["glu"][t_p:].reshape(bs, n_new, CONV_CH).transpose(1, 0, 2)], axis=0)
        new_s["conv"].append(ext_t[n_new:].transpose(1, 0, 2))
        cact_s = _conv_sample(ext_t, conv_w[l], _vec(conv_b[l]), _vec(conv_ln_g[l]), _vec(conv_ln_b[l]))
        cact_s = cact_s.transpose(1, 0, 2).reshape(n_s, CONV_CH)

        cact = jnp.concatenate([cact_p, cact_s], axis=0)
        lat = jnp.concatenate([lat_p, lat_s.reshape(n_s, MLA_HEADS * MLA_KV_LORA)], axis=0)
        sb = jnp.concatenate([sb_p, sb_s.reshape(n_s, N_HEADS * LANE)], axis=0)
        mb = jnp.concatenate([mb_p, mb_s.reshape(n_s, N_HEADS * LANE)], axis=0)
        x1 = _merge(x, cact, lat, sb, mb, pc["gate"], _vec(gate_b[l]),
                    w_conv_out[l].astype(BF16), wuv, w_mla_out[l].astype(BF16),
                    _head_pad_cols(w_sb_out[l].T).T.astype(BF16), _head_pad_cols(w_mb_out[l].T).T.astype(BF16),
                    w_out[l].astype(BF16), _vec(ln1_g[l]), _vec(ln1_b[l]), alpha)
        gate, xb = _router(x1, router_w[l], router_bias[l])
        x = _experts(x1, xb, gate, w_e_gate[l].astype(BF16), w_e_up[l].astype(BF16), w_e_down[l].astype(BF16),
                     w_s_gate[l].astype(BF16), w_s_up[l].astype(BF16), w_s_down[l].astype(BF16),
                     _vec(ln2_g[l]), _vec(ln2_b[l]), alpha)

    def stack_p(k, tail):
        return jnp.stack(new_p[k]).reshape((depth, 1, t_p) + tail)

    def stack_s(k, tail):
        return jnp.stack(new_s[k]).reshape((depth, bs, n_new) + tail)

    kv = (KV_HEADS, HEAD_DIM)
    return (x[:t_p].reshape(1, t_p, D_MODEL), x[t_p:].reshape(bs, n_new, D_MODEL),
            stack_p("ckv", (MLA_KV_LORA,)), stack_s("ckv", (MLA_KV_LORA,)),
            stack_p("kpe", (MLA_ROPE,)), stack_s("kpe", (MLA_ROPE,)),
            stack_p("sk", kv), stack_s("sk", kv), stack_p("sv", kv), stack_s("sv", kv),
            stack_p("mk", kv), stack_s("mk", kv), stack_p("mv", kv), stack_s("mv", kv),
            jnp.stack(new_p["conv"]).reshape(depth, 1, CONV_WIDTH - 1, CONV_CH),
            jnp.stack(new_s["conv"]))
```
